```python
import jax, jax.numpy as jnp
from jax import lax
import numpy as np

D_MODEL = 2048
BATCH = 4
SEQ = 2048
DEPTH = 4
DEC_BATCH = 128
DEC_SEQ = 8
PAST_LEN = 16384
PAGE_SIZE = 128

N_MIXERS = 2
N_RET = (DEPTH + 1) // 2
N_HGRN = DEPTH // 2
RET_HEADS = 8
RET_DK = D_MODEL // RET_HEADS
RET_DV = 2 * RET_DK
RET_KD = RET_HEADS * RET_DK
RET_VD = RET_HEADS * RET_DV
HG_EXPAND = 128
HG_HEADS = D_MODEL // HG_EXPAND
HG_DK = HG_EXPAND
HG_DV = D_MODEL // HG_HEADS
D_FF = -(-8 * D_MODEL // (3 * 256)) * 256
CHUNK = 64
ROPE_BASE = 10000.0
ALPHA = (2 * DEPTH) ** 0.25
BETA = (8 * DEPTH) ** -0.25
LN_EPS = 1e-5
LB_FLOOR = 1e-30
F32 = jnp.float32

kernel_name = 'retnet_hgrn2_hybrid_step'


def _chunk_len(L):
    return CHUNK if L % CHUNK == 0 else L


def _to_chunks(t, C):
    B, L = t.shape[:2]
    return jnp.moveaxis(t.reshape(B, L // C, C, *t.shape[2:]), 1, 0)


def _from_chunks(t):
    N, B, C = t.shape[:3]
    return jnp.moveaxis(t, 0, 1).reshape(B, N * C, *t.shape[3:])


def layer_norm(x, g, b):
    xf = x.astype(F32)
    mu = jnp.mean(xf, axis=-1, keepdims=True)
    var = jnp.mean(jnp.square(xf - mu), axis=-1, keepdims=True)
    return ((xf - mu) * lax.rsqrt(var + LN_EPS) * g.astype(F32) + b.astype(F32)).astype(x.dtype)


def rotary(t, pos):
    E = t.shape[-1]
    inv = 1.0 / (ROPE_BASE ** jnp.linspace(0.0, 1.0, E // 2, dtype=F32))
    ang = pos[:, None] * inv[None, :]
    cos = jnp.cos(ang)[None, :, None, :]
    sin = jnp.sin(ang)[None, :, None, :]
    t = t.astype(F32)
    t1, t2 = t[..., 0::2], t[..., 1::2]
    return jnp.stack([t1 * cos - t2 * sin, t1 * sin + t2 * cos], axis=-1).reshape(t.shape)


def retention_log_decay():
    return jnp.log(1.0 - 2.0 ** (-5.0 - jnp.arange(RET_HEADS, dtype=F32)))


def retention_chunked(q, k, v, s0):
    L = q.shape[1]
    C = _chunk_len(L)
    lg = retention_log_decay()
    idx = jnp.arange(C, dtype=F32)
    rel = idx[:, None] - idx[None, :]
    causal = rel >= 0
    decay = jnp.where(causal[None], jnp.exp(jnp.where(causal, rel, 0.0)[None] * lg[:, None, None]), 0.0)
    q_dec = jnp.exp((idx + 1)[:, None] * lg[None, :])
    k_dec = jnp.exp((C - 1 - idx)[:, None] * lg[None, :])
    c_dec = jnp.exp(C * lg)

    def step(s, inp):
        qc, kc, vc = inp
        a = jnp.einsum('bihd,bjhd->bhij', qc, kc) * decay[None]
        o = (jnp.einsum('bhij,bjhe->bihe', a, vc)
             + jnp.einsum('bihd,bhde->bihe', qc * q_dec[None, :, :, None], s))
        s = c_dec[None, :, None, None] * s + jnp.einsum('bjhd,bjhe->bhde', kc * k_dec[None, :, :, None], vc)
        return s, o

    s, o = lax.scan(step, s0.astype(F32),
                    (_to_chunks(q.astype(F32), C), _to_chunks(k.astype(F32), C), _to_chunks(v.astype(F32), C)))
    return _from_chunks(o), s


def gla_chunked(q, k, v, log_f, s0):
    L = q.shape[1]
    C = _chunk_len(L)
    ci = jnp.arange(C)
    mask = (ci[:, None] >= ci[None, :])[None, :, :, None, None]

    def step(s, inp):
        qc, kc, vc, gc = inp
        G = jnp.cumsum(gc, axis=1)
        diff = jnp.where(mask, G[:, :, None] - G[:, None], 0.0)
        w = jnp.where(mask, jnp.exp(diff), 0.0)
        a = jnp.sum(qc[:, :, None] * kc[:, None] * w, axis=-1)
        o = (jnp.einsum('bijh,bjhe->bihe', a, vc)
             + jnp.einsum('bihd,bhde->bihe', qc * jnp.exp(G), s))
        g_last = G[:, -1]
        s = jnp.exp(g_last)[..., None] * s + jnp.einsum('bjhd,bjhe->bhde', kc * jnp.exp(g_last[:, None] - G), vc)
        return s, o

    s, o = lax.scan(step, s0.astype(F32),
                    (_to_chunks(q, C), _to_chunks(k, C), _to_chunks(v.astype(F32), C), _to_chunks(log_f, C)))
    return _from_chunks(o), s


def retention_mixer(x, pos0, s0, w_in, gn_gain, w_out):
    Bx, L, _ = x.shape
    q, k, v, g = jnp.split(x @ w_in, [RET_KD, 2 * RET_KD, 2 * RET_KD + RET_VD], axis=-1)
    pos = pos0 + jnp.arange(L, dtype=F32)
    q = rotary(q.reshape(Bx, L, RET_HEADS, RET_DK), pos)
    k = rotary(k.reshape(Bx, L, RET_HEADS, RET_DK), pos) * (RET_DK ** -0.5)
    v = v.reshape(Bx, L, RET_HEADS, RET_DV)
    o, s = retention_chunked(q, k, v, s0)
    mu = jnp.mean(o, axis=-1, keepdims=True)
    var = jnp.mean(jnp.square(o - mu), axis=-1, keepdims=True)
    o = (o - mu) * lax.rsqrt(var + LN_EPS) * gn_gain.astype(F32).reshape(RET_HEADS, RET_DV)
    o = o.reshape(Bx, L, RET_VD)
    y = (jax.nn.silu(g.astype(F32)) * o).astype(x.dtype) @ w_out
    return y, s


def hgrn2_mixer(x, s0, lb, w_in, norm_gain, w_out):
    Bx, L, _ = x.shape
    q, fpre, i, g = jnp.split(x @ w_in, 4, axis=-1)
    q = jax.nn.silu(q.astype(F32)).reshape(Bx, L, HG_HEADS, HG_DK) * (HG_DK ** -0.5)
    fpre = fpre.astype(F32).reshape(Bx, L, HG_HEADS, HG_DK)
    lb = lb.reshape(HG_HEADS, HG_DK)
    log_lb = jnp.log(jnp.maximum(lb, LB_FLOOR))
    log_f = jnp.logaddexp(log_lb, jnp.log1p(-lb) + jax.nn.log_sigmoid(fpre))
    k = (1.0 - lb) * jax.nn.sigmoid(-fpre)
    v = i.reshape(Bx, L, HG_HEADS, HG_DV)
    o, s = gla_chunked(q, k, v, log_f, s0)
    o = o * lax.rsqrt(jnp.mean(jnp.square(o), axis=-1, keepdims=True) + LN_EPS) * norm_gain.astype(F32)
    o = o.reshape(Bx, L, D_MODEL) * jax.nn.silu(g.astype(F32))
    return o.astype(x.dtype) @ w_out, s


def swiglu(x, w_in, w_out):
    gate, up = jnp.split(x @ w_in, 2, axis=-1)
    return (jax.nn.silu(gate) * up) @ w_out


def setup_inputs(seed: int = 0) -> dict:
    key = jax.random.key(seed)
    ks = jax.random.split(key, 20)
    nrm = jax.random.normal
    ret_col = jnp.concatenate([jnp.ones((2 * RET_KD,), F32), jnp.full((RET_VD,), BETA, F32), jnp.ones((RET_VD,), F32)])
    hg_col = jnp.concatenate([jnp.ones((2 * D_MODEL,), F32), jnp.full((D_MODEL,), BETA, F32), jnp.ones((D_MODEL,), F32)])
    return {
        'x_prompt': nrm(ks[0], (BATCH, SEQ, D_MODEL), F32),
        'x_sample': nrm(ks[1], (DEC_BATCH, DEC_SEQ, D_MODEL), F32),
        'state_ret': 0.5 * nrm(ks[2], (N_RET, DEC_BATCH, RET_HEADS, RET_DK, RET_DV), F32),
        'state_hgrn': 0.5 * nrm(ks[3], (N_HGRN, DEC_BATCH, HG_HEADS, HG_DK, HG_DV), F32),
        'ret_w_in': nrm(ks[4], (N_RET, D_MODEL, 2 * RET_KD + 2 * RET_VD), F32) * (D_MODEL ** -0.5) * ret_col,
        'ret_gn_gain': 1.0 + 0.02 * nrm(ks[5], (N_RET, RET_VD), F32),
        'ret_w_out': nrm(ks[6], (N_RET, RET_VD, D_MODEL), F32) * (RET_VD ** -0.5) * BETA,
        'hgrn_w_in': nrm(ks[7], (N_HGRN, D_MODEL, 4 * D_MODEL), F32) * (D_MODEL ** -0.5) * hg_col,
        'hgrn_lb_logits': 0.1 * nrm(ks[8], (N_HGRN, D_MODEL), F32),
        'hgrn_norm_gain': 1.0 + 0.02 * nrm(ks[9], (N_HGRN, HG_DV), F32),
        'hgrn_w_out': nrm(ks[10], (N_HGRN, D_MODEL, D_MODEL), F32) * (D_MODEL ** -0.5) * BETA,
        'ln_mix_g': 1.0 + 0.02 * nrm(ks[11], (DEPTH, D_MODEL), F32),
        'ln_mix_b': 0.02 * nrm(ks[12], (DEPTH, D_MODEL), F32),
        'ffn_w_in': nrm(ks[13], (DEPTH, D_MODEL, 2 * D_FF), F32) * (D_MODEL ** -0.5) * BETA,
        'ffn_w_out': nrm(ks[14], (DEPTH, D_FF, D_MODEL), F32) * (D_FF ** -0.5) * BETA,
        'ln_ffn_g': 1.0 + 0.02 * nrm(ks[15], (DEPTH, D_MODEL), F32),
        'ln_ffn_b': 0.02 * nrm(ks[16], (DEPTH, D_MODEL), F32),
    }


def reference(x_prompt, x_sample, state_ret, state_hgrn, ret_w_in, ret_gn_gain, ret_w_out,
              hgrn_w_in, hgrn_lb_logits, hgrn_norm_gain, hgrn_w_out,
              ln_mix_g, ln_mix_b, ffn_w_in, ffn_w_out, ln_ffn_g, ln_ffn_b):
    p = jax.nn.softmax(hgrn_lb_logits.astype(F32), axis=0)
    lbs = jnp.cumsum(p, axis=0) - p[0]
    xp, xs = x_prompt, x_sample
    bp = x_prompt.shape[0]
    ret_p, ret_s, hg_p, hg_s = [], [], [], []
    for layer in range(DEPTH):
        j = layer // N_MIXERS
        if layer % N_MIXERS == 0:
            z = jnp.zeros((bp, RET_HEADS, RET_DK, RET_DV), F32)
            mp, sp = retention_mixer(xp, 0, z, ret_w_in[j], ret_gn_gain[j], ret_w_out[j])
            ms, ss = retention_mixer(xs, PAST_LEN, state_ret[j], ret_w_in[j], ret_gn_gain[j], ret_w_out[j])
            ret_p.append(sp.astype(state_ret.dtype))
            ret_s.append(ss.astype(state_ret.dtype))
        else:
            z = jnp.zeros((bp, HG_HEADS, HG_DK, HG_DV), F32)
            mp, sp = hgrn2_mixer(xp, z, lbs[j], hgrn_w_in[j], hgrn_norm_gain[j], hgrn_w_out[j])
            ms, ss = hgrn2_mixer(xs, state_hgrn[j], lbs[j], hgrn_w_in[j], hgrn_norm_gain[j], hgrn_w_out[j])
            hg_p.append(sp.astype(state_hgrn.dtype))
            hg_s.append(ss.astype(state_hgrn.dtype))
        xp = layer_norm(ALPHA * xp + mp, ln_mix_g[layer], ln_mix_b[layer])
        xs = layer_norm(ALPHA * xs + ms, ln_mix_g[layer], ln_mix_b[layer])
        xp = layer_norm(ALPHA * xp + swiglu(xp, ffn_w_in[layer], ffn_w_out[layer]), ln_ffn_g[layer], ln_ffn_b[layer])
        xs = layer_norm(ALPHA * xs + swiglu(xs, ffn_w_in[layer], ffn_w_out[layer]), ln_ffn_g[layer], ln_ffn_b[layer])
    return (xp, xs, jnp.stack(ret_p), jnp.stack(ret_s), jnp.stack(hg_p), jnp.stack(hg_s))
```

```python
import functools
import math

import jax
import jax.numpy as jnp
from jax import lax
from jax.experimental import pallas as pl
from jax.experimental.pallas import tpu as pltpu

F32 = jnp.float32
BF16 = jnp.bfloat16

D_MODEL = 2048
BATCH = 4
SEQ = 2048
DEPTH = 4
DEC_BATCH = 128
DEC_SEQ = 8
PAST_LEN = 16384
N_RET = 2
N_HGRN = 2
RET_HEADS = 8
RET_DK = 256
RET_DV = 512
RET_KD = RET_HEADS * RET_DK
RET_VD = RET_HEADS * RET_DV
HG_HEADS = 16
HG_DK = 128
HG_DV = 128
D_FF = 5632
ROPE_BASE = 10000.0
ALPHA = (2 * DEPTH) ** 0.25
LN_EPS = 1e-5
LB_FLOOR = 1e-30

MP_ROWS = BATCH * SEQ
MS_ROWS = DEC_BATCH * DEC_SEQ
M_ROWS = MP_ROWS + MS_ROWS

VMEM_LIMIT_BYTES = 56 * 1024 * 1024

TM = 1024
TN = 512
TM_LN = 512
TK_LN = 512

RET_C = 256
GLA_C = 128
GLA_SUB = 16
GLA_HB = 4
SAMPLE_BB = 2
RET_S_HB = 4


def _params(*sem):
    return pltpu.CompilerParams(dimension_semantics=sem, vmem_limit_bytes=VMEM_LIMIT_BYTES)


def _sigmoid(x):
    return 1.0 / (1.0 + jnp.exp(-x))


def _silu(x):
    return x * _sigmoid(x)


def _dense_body(*refs, n_w, n_extra, n_out, epilogue):
    x_ref = refs[0]
    w_refs = refs[1:1 + n_w]
    extra_refs = refs[1 + n_w:1 + n_w + n_extra]
    out_refs = refs[1 + n_w + n_extra:1 + n_w + n_extra + n_out]
    wb_refs = refs[1 + n_w + n_extra + n_out:]

    @pl.when(pl.program_id(1) == 0)
    def _():
        for w_ref, wb_ref in zip(w_refs, wb_refs):
            wb_ref[...] = w_ref[...].astype(BF16)

    x = x_ref[...]
    accs = [jnp.dot(x, wb_ref[...], preferred_element_type=F32) for wb_ref in wb_refs]
    epilogue(accs, extra_refs, out_refs)


def _dense(x, w, layer, col_offs, n_cols, epilogue, out_dtypes, extras=(), extra_specs=(), name="dense"):
    m, k = x.shape
    assert m % TM == 0 and n_cols % TN == 0 and all(c % TN == 0 for c in col_offs)
    grid = (n_cols // TN, m // TM)
    in_specs = [pl.BlockSpec((TM, k), lambda j, i: (i, 0))]
    for off in col_offs:
        in_specs.append(pl.BlockSpec((None, k, TN),
                                     functools.partial(lambda j, i, o: (layer, 0, j + o), o=off // TN)))
    in_specs += list(extra_specs)
    out_specs = [pl.BlockSpec((TM, TN), lambda j, i: (i, j)) for _ in out_dtypes]
    out_shape = [jax.ShapeDtypeStruct((m, n_cols), dt) for dt in out_dtypes]
    body = functools.partial(_dense_body, n_w=len(col_offs), n_extra=len(extras),
                             n_out=len(out_dtypes), epilogue=epilogue)
    return pl.pallas_call(
        body, grid=grid, in_specs=in_specs, out_specs=out_specs, out_shape=out_shape,
        scratch_shapes=[pltpu.VMEM((k, TN), BF16) for _ in col_offs],
        compiler_params=_params("arbitrary", "arbitrary"), name=name,
    )(x, *([w] * len(col_offs)), *extras)


def _ep_plain(accs, extra_refs, out_refs):
    out_refs[0][...] = accs[0].astype(out_refs[0].dtype)


def _ep_silu(accs, extra_refs, out_refs):
    out_refs[0][...] = _silu(accs[0]).astype(out_refs[0].dtype)


def _ep_silu_scaled(accs, extra_refs, out_refs, *, scale):
    out_refs[0][...] = (_silu(accs[0]) * scale).astype(out_refs[0].dtype)


def _ep_swiglu(accs, extra_refs, out_refs):
    out_refs[0][...] = (_silu(accs[0]) * accs[1]).astype(out_refs[0].dtype)


def _ep_rotary(accs, extra_refs, out_refs, *, scale):
    cos_ref, sin_ref = extra_refs
    t = accs[0]
    lane = lax.broadcasted_iota(jnp.int32, t.shape, 1)
    swapped = jnp.where((lane & 1) == 0,
                        pltpu.roll(t, t.shape[1] - 1, axis=1),
                        pltpu.roll(t, 1, axis=1))
    cos = cos_ref[...]
    sin = sin_ref[...]
    for h in range(TN // RET_DK):
        sl = slice(h * RET_DK, (h + 1) * RET_DK)
        r = t[:, sl] * cos + swapped[:, sl] * sin
        if scale != 1.0:
            r = r * scale
        out_refs[0][:, sl] = r.astype(out_refs[0].dtype)


def _ep_forget(accs, extra_refs, out_refs, *, layer):
    logits_ref = extra_refs[0]
    rows = [logits_ref[r:r + 1, :] for r in range(N_HGRN)]
    mx = functools.reduce(jnp.maximum, rows)
    es = [jnp.exp(r - mx) for r in rows]
    den = functools.reduce(lambda a, b: a + b, es)
    ps = [e / den for e in es]
    lb = functools.reduce(lambda a, b: a + b, ps[:layer + 1]) - ps[0]
    fpre = accs[0]
    log_lb = jnp.log(jnp.maximum(lb, LB_FLOOR))
    log_sig = jnp.minimum(fpre, 0.0) - jnp.log1p(jnp.exp(-jnp.abs(fpre)))
    b = jnp.log1p(-lb) + log_sig
    log_f = jnp.maximum(log_lb, b) + jnp.log1p(jnp.exp(-jnp.abs(log_lb - b)))
    out_refs[0][...] = log_f
    out_refs[1][...] = (1.0 - lb) * _sigmoid(-fpre)


def _dense_ln_body(h_ref, w_ref, x_ref, g_ref, b_ref, of_ref, ob_ref, *, n_k):
    kk = pl.program_id(1)
    part = jnp.dot(h_ref[...], w_ref[...], preferred_element_type=F32)

    @pl.when(kk == 0)
    def _():
        of_ref[...] = ALPHA * x_ref[...] + part

    @pl.when(kk > 0)
    def _():
        of_ref[...] += part

    @pl.when(kk == n_k - 1)
    def _():
        y = of_ref[...]
        mu = jnp.mean(y, axis=-1, keepdims=True)
        d = y - mu
        var = jnp.mean(d * d, axis=-1, keepdims=True)
        r = d * lax.rsqrt(var + LN_EPS) * g_ref[...] + b_ref[...]
        of_ref[...] = r
        ob_ref[...] = r.astype(BF16)


def _dense_ln(h, w_bf16, layer, x_res, gain, bias, name):
    m, k = h.shape
    d = w_bf16.shape[2]
    assert m % TM_LN == 0 and k % TK_LN == 0
    n_k = k // TK_LN
    return pl.pallas_call(
        functools.partial(_dense_ln_body, n_k=n_k),
        grid=(m // TM_LN, n_k),
        in_specs=[pl.BlockSpec((TM_LN, TK_LN), lambda i, kk: (i, kk)),
                  pl.BlockSpec((None, TK_LN, d), lambda i, kk: (layer, kk, 0)),
                  pl.BlockSpec((TM_LN, d), lambda i, kk: (i, 0)),
                  pl.BlockSpec((1, d), lambda i, kk: (0, 0)),
                  pl.BlockSpec((1, d), lambda i, kk: (0, 0))],
        out_specs=[pl.BlockSpec((TM_LN, d), lambda i, kk: (i, 0)),
                   pl.BlockSpec((TM_LN, d), lambda i, kk: (i, 0))],
        out_shape=[jax.ShapeDtypeStruct((m, d), F32), jax.ShapeDtypeStruct((m, d), BF16)],
        compiler_params=_params("parallel", "arbitrary"), name=name,
    )(h, w_bf16, x_res, gain.reshape(1, d), bias.reshape(1, d))


def _group_norm_gate(o, gain, gate):
    mu = jnp.mean(o, axis=-1, keepdims=True)
    d = o - mu
    var = jnp.mean(d * d, axis=-1, keepdims=True)
    return gate * (d * lax.rsqrt(var + LN_EPS) * gain)


def _ret_step(q, k, v, s, dmat, qdec, kdec, cdec):
    a = lax.dot_general(q.astype(BF16), k.astype(BF16), (((1,), (1,)), ((), ())),
                        preferred_element_type=F32) * dmat
    o = jnp.dot(a.astype(BF16), v, preferred_element_type=F32)
    o = o + jnp.dot((q * qdec).astype(BF16), s.astype(BF16), preferred_element_type=F32)
    s_new = cdec * s + lax.dot_general((k * kdec).astype(BF16), v, (((0,), (0,)), ((), ())),
                                       preferred_element_type=F32)
    return o, s_new


def _ret_prompt_body(q_ref, k_ref, v_ref, g_ref, gain_ref, dmat_ref, qdec_ref, kdec_ref, cdec_ref,
                     o_ref, s_ref):
    @pl.when(pl.program_id(2) == 0)
    def _():
        s_ref[...] = jnp.zeros_like(s_ref)

    o, s_new = _ret_step(q_ref[...], k_ref[...], v_ref[...], s_ref[0, 0],
                         dmat_ref[0], qdec_ref[0], kdec_ref[0], cdec_ref[0])
    s_ref[0, 0] = s_new
    o_ref[...] = _group_norm_gate(o, gain_ref[...], g_ref[...]).astype(BF16)


def _ret_decay_tables(c):
    lg = jnp.log(1.0 - 2.0 ** (-5.0 - jnp.arange(RET_HEADS, dtype=F32)))
    idx = jnp.arange(c, dtype=F32)
    rel = idx[:, None] - idx[None, :]
    causal = rel >= 0
    dmat = jnp.where(causal[None], jnp.exp(jnp.where(causal, rel, 0.0)[None] * lg[:, None, None]), 0.0)
    qdec = jnp.exp((idx + 1)[None, :] * lg[:, None])
    kdec = jnp.exp((c - 1 - idx)[None, :] * lg[:, None])
    cdec = jnp.exp(c * lg)
    qdec = jnp.broadcast_to(qdec[:, :, None], (RET_HEADS, c, RET_DK))
    kdec = jnp.broadcast_to(kdec[:, :, None], (RET_HEADS, c, RET_DK))
    cdec = jnp.broadcast_to(cdec[:, None, None], (RET_HEADS, 1, RET_DV))
    return dmat, qdec, kdec, cdec


def _ret_prompt(q, k, v, g, gain):
    c = RET_C
    nc = SEQ // c
    dmat, qdec, kdec, cdec = _ret_decay_tables(c)
    row = lambda b, h, ci: (b * nc + ci, h)
    head = lambda b, h, ci: (h, 0, 0)
    return pl.pallas_call(
        _ret_prompt_body,
        grid=(BATCH, RET_HEADS, nc),
        in_specs=[pl.BlockSpec((c, RET_DK), row), pl.BlockSpec((c, RET_DK), row),
                  pl.BlockSpec((c, RET_DV), row), pl.BlockSpec((c, RET_DV), row),
                  pl.BlockSpec((1, RET_DV), lambda b, h, ci: (0, h)),
                  pl.BlockSpec((1, c, c), head), pl.BlockSpec((1, c, RET_DK), head),
                  pl.BlockSpec((1, c, RET_DK), head), pl.BlockSpec((1, 1, RET_DV), head)],
        out_specs=[pl.BlockSpec((c, RET_DV), row),
                   pl.BlockSpec((1, 1, RET_DK, RET_DV), lambda b, h, ci: (b, h, 0, 0))],
        out_shape=[jax.ShapeDtypeStruct((M_ROWS, RET_VD), BF16),
                   jax.ShapeDtypeStruct((BATCH, RET_HEADS, RET_DK, RET_DV), F32)],
        compiler_params=_params("parallel", "parallel", "arbitrary"), name="ret_prompt",
    )(q, k, v, g, gain.reshape(1, RET_VD), dmat, qdec, kdec, cdec)


def _ret_sample_body(q_ref, k_ref, v_ref, g_ref, gain_ref, dmat_ref, qdec_ref, kdec_ref, cdec_ref,
                     s0_ref, o_prev_ref, st_prev_ref, o_ref, s_ref):
    del o_prev_ref, st_prev_ref
    hb = pl.program_id(1)
    v_all = v_ref[...].astype(F32)
    outs = []
    for bi in range(SAMPLE_BB):
        rows = slice(bi * DEC_SEQ, (bi + 1) * DEC_SEQ)
        per_head = []
        for h in range(RET_S_HB):
            ks = slice(h * RET_DK, (h + 1) * RET_DK)
            vs = slice(h * RET_DV, (h + 1) * RET_DV)
            o, s_new = _ret_step(q_ref[rows, ks], k_ref[rows, ks], v_all[rows, vs].astype(BF16),
                                 s0_ref[0, bi, h], dmat_ref[h], qdec_ref[h], kdec_ref[h], cdec_ref[h])
            s_ref[0, bi, h] = s_new
            per_head.append(_group_norm_gate(o, gain_ref[:, vs], g_ref[rows, vs]))
        outs.append(jnp.concatenate(per_head, axis=1))
    del hb
    o_ref[...] = jnp.concatenate(outs, axis=0).astype(BF16)


def _ret_sample(q, k, v, g, gain, state_all, layer_j, o_prev, st_prev):
    c = DEC_SEQ
    dmat, qdec, kdec, cdec = _ret_decay_tables(c)
    rows_per = SAMPLE_BB * DEC_SEQ
    row0 = MP_ROWS // rows_per
    n_hb = RET_HEADS // RET_S_HB
    row = lambda bb, hb: (row0 + bb, hb)
    hsel = lambda bb, hb: (hb, 0, 0)
    st_block = (1, SAMPLE_BB, RET_S_HB, RET_DK, RET_DV)
    st_map = lambda bb, hb: (layer_j, bb, hb, 0, 0)
    in_specs = [pl.BlockSpec((rows_per, RET_S_HB * RET_DK), row),
                pl.BlockSpec((rows_per, RET_S_HB * RET_DK), row),
                pl.BlockSpec((rows_per, RET_S_HB * RET_DV), row),
                pl.BlockSpec((rows_per, RET_S_HB * RET_DV), row),
                pl.BlockSpec((1, RET_S_HB * RET_DV), lambda bb, hb: (0, hb)),
                pl.BlockSpec((RET_S_HB, c, c), hsel),
                pl.BlockSpec((RET_S_HB, c, RET_DK), hsel),
                pl.BlockSpec((RET_S_HB, c, RET_DK), hsel),
                pl.BlockSpec((RET_S_HB, 1, RET_DV), hsel),
                pl.BlockSpec(st_block, st_map),
                pl.BlockSpec(memory_space=pl.ANY)]
    operands = [q, k, v, g, gain.reshape(1, RET_VD), dmat, qdec, kdec, cdec, state_all, o_prev]
    aliases = {10: 0}
    if st_prev is not None:
        in_specs.append(pl.BlockSpec(memory_space=pl.ANY))
        operands.append(st_prev)
        aliases[11] = 1
        body = _ret_sample_body
    else:
        body = lambda *refs: _ret_sample_body(*refs[:11], None, *refs[11:])
    return pl.pallas_call(
        body,
        grid=(DEC_BATCH // SAMPLE_BB, n_hb),
        in_specs=in_specs,
        out_specs=[pl.BlockSpec((rows_per, RET_S_HB * RET_DV), row), pl.BlockSpec(st_block, st_map)],
        out_shape=[jax.ShapeDtypeStruct((M_ROWS, RET_VD), BF16),
                   jax.ShapeDtypeStruct((N_RET, DEC_BATCH, RET_HEADS, RET_DK, RET_DV), F32)],
        input_output_aliases=aliases,
        compiler_params=_params("parallel", "parallel"), name="ret_sample",
    )(*operands)


def _rms_norm_gate(o, gain, gate):
    return o * lax.rsqrt(jnp.mean(o * o, axis=-1, keepdims=True) + LN_EPS) * gain * gate


def _exact_diag_block(gq, qq, kk, lane_base, ones_b):
    n = gq.shape[0]
    tiles = []
    for j in range(n):
        e = jnp.exp(jnp.minimum(gq - gq[j:j + 1, :], 0.0))
        tiles.append(e * qq * kk[j:j + 1, :])
    p = jnp.concatenate(tiles, axis=0).astype(BF16)
    sums = jnp.dot(p, ones_b, preferred_element_type=F32)
    lanes = sums.shape[1]
    rel = lax.broadcasted_iota(jnp.int32, (n, lanes), 1) - lane_base
    row = lax.broadcasted_iota(jnp.int32, (n, lanes), 0)
    acc = jnp.zeros((n, lanes), F32)
    for j in range(n):
        acc = jnp.where(rel == j, sums[j * n:(j + 1) * n, :], acc)
    return jnp.where(row >= rel, acc, 0.0)


def _gla_prompt_body(q_ref, k_ref, lf_ref, v_ref, g_ref, gain_ref, ltri_ref, o_ref, st_ref, s_scr):
    ci = pl.program_id(2)
    c = GLA_C

    @pl.when(ci == 0)
    def _():
        s_scr[...] = jnp.zeros_like(s_scr)

    ltri = ltri_ref[...]
    ones_b = jnp.ones((HG_DK, c), BF16)
    row = lax.broadcasted_iota(jnp.int32, (c, HG_DK), 0)
    ri = lax.broadcasted_iota(jnp.int32, (c, c), 0)
    cj = lax.broadcasted_iota(jnp.int32, (c, c), 1)
    levels = []
    s = c // 2
    while s >= GLA_SUB:
        levels.append(s)
        s //= 2

    outs = []
    for hh in range(GLA_HB):
        sl = slice(hh * HG_DK, (hh + 1) * HG_DK)
        q = q_ref[:, sl]
        k = k_ref[:, sl]
        v = v_ref[:, sl]
        g_cum = jnp.dot(ltri, lf_ref[:, sl], precision=lax.Precision.HIGHEST,
                        preferred_element_type=F32)
        s_t = s_scr[hh]

        a = jnp.zeros((c, c), F32)
        for s in levels:
            ref_rows = [jnp.broadcast_to(g_cum[m + s - 1:m + s, :], (2 * s, HG_DK))
                        for m in range(0, c, 2 * s)]
            ref = ref_rows[0] if len(ref_rows) == 1 else jnp.concatenate(ref_rows, axis=0)
            e = jnp.exp(-jnp.abs(g_cum - ref))
            second = (row & s) != 0
            qt = jnp.where(second, q * e, 0.0).astype(BF16)
            kt = jnp.where(second, 0.0, k * e).astype(BF16)
            p = lax.dot_general(qt, kt, (((1,), (1,)), ((), ())), preferred_element_type=F32)
            if 2 * s < c:
                sh = int(math.log2(2 * s))
                p = jnp.where((ri >> sh) == (cj >> sh), p, 0.0)
            a = a + p
        diag = [_exact_diag_block(g_cum[m:m + GLA_SUB, :], q[m:m + GLA_SUB, :], k[m:m + GLA_SUB, :],
                                  m, ones_b) for m in range(0, c, GLA_SUB)]
        a = a + jnp.concatenate(diag, axis=0)

        o = jnp.dot(a.astype(BF16), v, preferred_element_type=F32)
        o = o + lax.dot_general((q * jnp.exp(g_cum)).astype(BF16), s_t.astype(BF16),
                                (((1,), (1,)), ((), ())), preferred_element_type=F32)
        g_last = g_cum[c - 1:c, :]
        kd = (k * jnp.exp(g_last - g_cum)).astype(BF16)
        s_scr[hh] = jnp.exp(g_last) * s_t + lax.dot_general(
            v, kd, (((0,), (0,)), ((), ())), preferred_element_type=F32)
        outs.append(_rms_norm_gate(o, gain_ref[...], g_ref[:, sl]))

    o_ref[...] = jnp.concatenate(outs, axis=1).astype(BF16)

    @pl.when(ci == pl.num_programs(2) - 1)
    def _():
        for hh in range(GLA_HB):
            st_ref[0, hh] = s_scr[hh].T


def _gla_prompt(q, k, lf, v, g, gain):
    c = GLA_C
    nc = SEQ // c
    w = GLA_HB * HG_DK
    ltri = (jnp.arange(c)[:, None] >= jnp.arange(c)[None, :]).astype(F32)
    row = lambda b, hb, ci: (b * nc + ci, hb)
    return pl.pallas_call(
        _gla_prompt_body,
        grid=(BATCH, HG_HEADS // GLA_HB, nc),
        in_specs=[pl.BlockSpec((c, w), row), pl.BlockSpec((c, w), row), pl.BlockSpec((c, w), row),
                  pl.BlockSpec((c, w), row), pl.BlockSpec((c, w), row),
                  pl.BlockSpec((1, HG_DV), lambda b, hb, ci: (0, 0)),
                  pl.BlockSpec((c, c), lambda b, hb, ci: (0, 0))],
        out_specs=[pl.BlockSpec((c, w), row),
                   pl.BlockSpec((1, GLA_HB, HG_DK, HG_DV), lambda b, hb, ci: (b, hb, 0, 0))],
        out_shape=[jax.ShapeDtypeStruct((M_ROWS, D_MODEL), BF16),
                   jax.ShapeDtypeStruct((BATCH, HG_HEADS, HG_DK, HG_DV), F32)],
        scratch_shapes=[pltpu.VMEM((GLA_HB, HG_DV, HG_DK), F32)],
        compiler_params=_params("parallel", "parallel", "arbitrary"), name="gla_prompt",
    )(q, k, lf, v, g, gain.reshape(1, HG_DV), ltri)


def _gla_sample_body(q_ref, k_ref, lf_ref, v_ref, g_ref, gain_ref, s0_ref, o_prev_ref, st_prev_ref,
                     o_ref, s_ref):
    del o_prev_ref, st_prev_ref
    n = DEC_SEQ
    rows_per = SAMPLE_BB * n
    width = q_ref.shape[1]
    rix = lax.broadcasted_iota(jnp.int32, (rows_per, width), 0) & (n - 1)
    g_all = lf_ref[...]
    sh = 1
    while sh < n:
        g_all = g_all + jnp.where(rix >= sh, pltpu.roll(g_all, sh, axis=0), 0.0)
        sh *= 2
    v_all = v_ref[...].astype(F32)
    ones_b = jnp.ones((HG_DK, HG_DK), BF16)
    zpad = jnp.zeros((HG_DK - n, HG_DV), F32)
    outs = []
    for bi in range(SAMPLE_BB):
        rows = slice(bi * n, (bi + 1) * n)
        per_head = []
        for h in range(HG_HEADS):
            sl = slice(h * HG_DK, (h + 1) * HG_DK)
            q = q_ref[rows, sl]
            k = k_ref[rows, sl]
            g_cum = g_all[rows, sl]
            s0 = s0_ref[0, bi, h]
            a = _exact_diag_block(g_cum, q, k, 0, ones_b)
            v_pad = jnp.concatenate([v_all[rows, sl], zpad], axis=0).astype(BF16)
            o = jnp.dot(a.astype(BF16), v_pad, preferred_element_type=F32)
            o = o + jnp.dot((q * jnp.exp(g_cum)).astype(BF16), s0.astype(BF16),
                            preferred_element_type=F32)
            g_last = g_cum[n - 1:n, :]
            kd = k * jnp.exp(g_last - g_cum)
            kd_pad = jnp.concatenate([kd, zpad], axis=0).astype(BF16)
            decay_col = jnp.broadcast_to(jnp.exp(g_last), (HG_DV, HG_DK)).T
            s_ref[0, bi, h] = decay_col * s0 + lax.dot_general(
                kd_pad, v_pad, (((0,), (0,)), ((), ())), preferred_element_type=F32)
            per_head.append(_rms_norm_gate(o, gain_ref[...], g_ref[rows, sl]))
        outs.append(jnp.concatenate(per_head, axis=1))
    o_ref[...] = jnp.concatenate(outs, axis=0).astype(BF16)


def _gla_sample(q, k, lf, v, g, gain, state_all, layer_j, o_prev, st_prev):
    rows_per = SAMPLE_BB * DEC_SEQ
    row0 = MP_ROWS // rows_per
    row = lambda bb: (row0 + bb, 0)
    st_block = (1, SAMPLE_BB, HG_HEADS, HG_DK, HG_DV)
    st_map = lambda bb: (layer_j, bb, 0, 0, 0)
    in_specs = [pl.BlockSpec((rows_per, D_MODEL), row)] * 5 + [
        pl.BlockSpec((1, HG_DV), lambda bb: (0, 0)),
        pl.BlockSpec(st_block, st_map),
        pl.BlockSpec(memory_space=pl.ANY)]
    operands = [q, k, lf, v, g, gain.reshape(1, HG_DV), state_all, o_prev]
    aliases = {7: 0}
    if st_prev is not None:
        in_specs.append(pl.BlockSpec(memory_space=pl.ANY))
        operands.append(st_prev)
        aliases[8] = 1
        body = _gla_sample_body
    else:
        body = lambda *refs: _gla_sample_body(*refs[:8], None, *refs[8:])
    return pl.pallas_call(
        body,
        grid=(DEC_BATCH // SAMPLE_BB,),
        in_specs=in_specs,
        out_specs=[pl.BlockSpec((rows_per, D_MODEL), row), pl.BlockSpec(st_block, st_map)],
        out_shape=[jax.ShapeDtypeStruct((M_ROWS, D_MODEL), BF16),
                   jax.ShapeDtypeStruct((N_HGRN, DEC_BATCH, HG_HEADS, HG_DK, HG_DV), F32)],
        input_output_aliases=aliases,
        compiler_params=_params("parallel"), name="gla_sample",
    )(*operands)


def _rope_tables():
    inv = 1.0 / (ROPE_BASE ** jnp.linspace(0.0, 1.0, RET_DK // 2, dtype=F32))
    pos = jnp.concatenate([jnp.tile(0 + jnp.arange(SEQ, dtype=F32), BATCH),
                           jnp.tile(PAST_LEN + jnp.arange(DEC_SEQ, dtype=F32), DEC_BATCH)])
    ang = pos[:, None] * inv[None, :]
    cos = jnp.repeat(jnp.cos(ang), 2, axis=-1)
    sin = jnp.repeat(jnp.sin(ang), 2, axis=-1)
    sign = jnp.tile(jnp.array([-1.0, 1.0], F32), RET_DK // 2)
    return cos, sin * sign[None, :]


def _retention_layer(xb, w_in, gn_gain, state_all, layer_j, st_prev, rope):
    cos, sin = rope
    rope_specs = [pl.BlockSpec((TM, RET_DK), lambda j, i: (i, 0))] * 2
    (q,) = _dense(xb, w_in, layer_j, [0], RET_KD, functools.partial(_ep_rotary, scale=1.0), [F32],
                  extras=(cos, sin), extra_specs=rope_specs, name="ret_q")
    (k,) = _dense(xb, w_in, layer_j, [RET_KD], RET_KD,
                  functools.partial(_ep_rotary, scale=RET_DK ** -0.5), [F32],
                  extras=(cos, sin), extra_specs=rope_specs, name="ret_k")
    (v,) = _dense(xb, w_in, layer_j, [2 * RET_KD], RET_VD, _ep_plain, [BF16], name="ret_v")
    (g,) = _dense(xb, w_in, layer_j, [2 * RET_KD + RET_VD], RET_VD, _ep_silu, [F32], name="ret_g")
    o, st_p = _ret_prompt(q, k, v, g, gn_gain)
    o, st_s = _ret_sample(q, k, v, g, gn_gain, state_all, layer_j, o, st_prev)
    return o, st_p, st_s


def _hgrn_layer(xb, w_in, lb_logits, norm_gain, state_all, layer_j, st_prev):
    d = D_MODEL
    logit_spec = [pl.BlockSpec((N_HGRN, TN), lambda j, i: (0, j))]
    (q,) = _dense(xb, w_in, layer_j, [0], d, functools.partial(_ep_silu_scaled, scale=HG_DK ** -0.5),
                  [F32], name="hg_q")
    lf, k = _dense(xb, w_in, layer_j, [d], d, functools.partial(_ep_forget, layer=layer_j), [F32, F32],
                   extras=(lb_logits,), extra_specs=logit_spec, name="hg_f")
    (v,) = _dense(xb, w_in, layer_j, [2 * d], d, _ep_plain, [BF16], name="hg_v")
    (g,) = _dense(xb, w_in, layer_j, [3 * d], d, _ep_silu, [F32], name="hg_g")
    o, st_p = _gla_prompt(q, k, lf, v, g, norm_gain)
    o, st_s = _gla_sample(q, k, lf, v, g, norm_gain, state_all, layer_j, o, st_prev)
    return o, st_p, st_s


def kernel(x_prompt, x_sample, state_ret, state_hgrn, ret_w_in, ret_gn_gain, ret_w_out,
           hgrn_w_in, hgrn_lb_logits, hgrn_norm_gain, hgrn_w_out,
           ln_mix_g, ln_mix_b, ffn_w_in, ffn_w_out, ln_ffn_g, ln_ffn_b):
    assert x_prompt.shape == (BATCH, SEQ, D_MODEL) and x_sample.shape == (DEC_BATCH, DEC_SEQ, D_MODEL)
    assert state_ret.shape == (N_RET, DEC_BATCH, RET_HEADS, RET_DK, RET_DV)
    assert state_hgrn.shape == (N_HGRN, DEC_BATCH, HG_HEADS, HG_DK, HG_DV)
    assert ffn_w_in.shape == (DEPTH, D_MODEL, 2 * D_FF)

    x = jnp.concatenate([x_prompt.reshape(MP_ROWS, D_MODEL), x_sample.reshape(MS_ROWS, D_MODEL)], axis=0)
    xb = x.astype(BF16)
    rope = _rope_tables()
    ret_w_out_b = ret_w_out.astype(BF16)
    hgrn_w_out_b = hgrn_w_out.astype(BF16)
    ffn_w_out_b = ffn_w_out.astype(BF16)

    ret_p, hg_p = [], []
    ret_s = None
    hg_s = None
    for layer in range(DEPTH):
        j = layer // 2
        if layer % 2 == 0:
            o, st_p, ret_s = _retention_layer(xb, ret_w_in, ret_gn_gain[j], state_ret, j, ret_s, rope)
            ret_p.append(st_p)
            w_out = ret_w_out_b
        else:
            o, st_p, hg_s = _hgrn_layer(xb, hgrn_w_in, hgrn_lb_logits, hgrn_norm_gain[j],
                                        state_hgrn, j, hg_s)
            hg_p.append(st_p)
            w_out = hgrn_w_out_b
        x, xb = _dense_ln(o, w_out, j, x, ln_mix_g[layer], ln_mix_b[layer], name="mix_out")
        (h,) = _dense(xb, ffn_w_in, layer, [0, D_FF], D_FF, _ep_swiglu, [BF16], name="ffn_in")
        x, xb = _dense_ln(h, ffn_w_out_b, layer, x, ln_ffn_g[layer], ln_ffn_b[layer], name="ffn_out")

    return (x[:MP_ROWS].reshape(BATCH, SEQ, D_MODEL),
            x[MP_ROWS:].reshape(DEC_BATCH, DEC_SEQ, D_MODEL),
            jnp.stack(ret_p), ret_s, jnp.stack(hg_p), hg_s)
```

```python
import functools
import math

import jax
import jax.numpy as jnp
from jax import lax
from jax.experimental import pallas as pl
from jax.experimental.pallas import tpu as pltpu

F32 = jnp.float32
BF16 = jnp.bfloat16

D_MODEL = 2048
BATCH = 4
SEQ = 2048
DEPTH = 4
DEC_BATCH = 128
DEC_SEQ = 8
PAST_LEN = 16384
N_RET = 2
N_HGRN = 2
RET_HEADS = 8
RET_DK = 256
RET_DV = 512
RET_KD = RET_HEADS * RET_DK
RET_VD = RET_HEADS * RET_DV
HG_HEADS = 16
HG_DK = 128
HG_DV = 128
D_FF = 5632
ROPE_BASE = 10000.0
ALPHA = (2 * DEPTH) ** 0.25
LN_EPS = 1e-5
LB_FLOOR = 1e-30
LOG2_E = 1.4426950408889634

MP_ROWS = BATCH * SEQ
MS_ROWS = DEC_BATCH * DEC_SEQ
M_ROWS = MP_ROWS + MS_ROWS

VMEM_LIMIT_BYTES = 56 * 1024 * 1024

LANES = 128

TM = 1024
TN_WIDE = 1024
TN = 512
TM_LN = 512

RET_C = 256
GLA_C = 128
GLA_SUB = 8
GLA_HB = 4
SAMPLE_BB = 2
RET_S_HB = 4


def _params(*sem):
    return pltpu.CompilerParams(dimension_semantics=sem, vmem_limit_bytes=VMEM_LIMIT_BYTES)


def _sigmoid(x):
    return 1.0 / (1.0 + jnp.exp(-x))


def _silu(x):
    return x * _sigmoid(x)


def _dense_body(*refs, n_w, n_extra, n_out, epilogue):
    x_ref = refs[0]
    w_refs = refs[1:1 + n_w]
    extra_refs = refs[1 + n_w:1 + n_w + n_extra]
    out_refs = refs[1 + n_w + n_extra:1 + n_w + n_extra + n_out]
    wb_refs = refs[1 + n_w + n_extra + n_out:]

    @pl.when(pl.program_id(1) == 0)
    def _():
        for w_ref, wb_ref in zip(w_refs, wb_refs):
            wb_ref[...] = w_ref[...].astype(BF16)

    x = x_ref[...]
    accs = [jnp.dot(x, wb_ref[...], preferred_element_type=F32) for wb_ref in wb_refs]
    epilogue(accs, extra_refs, out_refs)


def _dense(x, w, layer, col_offs, n_cols, epilogue, out_dtypes, extras=(), extra_specs=(), tn=TN,
           name="dense"):
    m, k = x.shape
    assert m % TM == 0 and n_cols % tn == 0 and all(c % tn == 0 for c in col_offs)
    grid = (n_cols // tn, m // TM)
    in_specs = [pl.BlockSpec((TM, k), lambda j, i: (i, 0))]
    for off in col_offs:
        in_specs.append(pl.BlockSpec((None, k, tn),
                                     functools.partial(lambda j, i, o: (layer, 0, j + o), o=off // tn)))
    in_specs += list(extra_specs)
    out_specs = [pl.BlockSpec((TM, tn), lambda j, i: (i, j)) for _ in out_dtypes]
    out_shape = [jax.ShapeDtypeStruct((m, n_cols), dt) for dt in out_dtypes]
    body = functools.partial(_dense_body, n_w=len(col_offs), n_extra=len(extras),
                             n_out=len(out_dtypes), epilogue=epilogue)
    return pl.pallas_call(
        body, grid=grid, in_specs=in_specs, out_specs=out_specs, out_shape=out_shape,
        scratch_shapes=[pltpu.VMEM((k, tn), BF16) for _ in col_offs],
        compiler_params=_params("arbitrary", "arbitrary"), name=name,
    )(x, *([w] * len(col_offs)), *extras)


def _ep_plain(accs, extra_refs, out_refs):
    out_refs[0][...] = accs[0].astype(out_refs[0].dtype)


def _ep_silu(accs, extra_refs, out_refs):
    out_refs[0][...] = _silu(accs[0]).astype(out_refs[0].dtype)


def _ep_silu_scaled(accs, extra_refs, out_refs, *, scale):
    out_refs[0][...] = (_silu(accs[0]) * scale).astype(out_refs[0].dtype)


def _ep_swiglu(accs, extra_refs, out_refs):
    out_refs[0][...] = (_silu(accs[0]) * accs[1]).astype(out_refs[0].dtype)


def _ep_rotary(accs, extra_refs, out_refs, *, scale):
    cos_ref, sin_ref = extra_refs
    t = accs[0]
    lane = lax.broadcasted_iota(jnp.int32, t.shape, 1)
    swapped = jnp.where((lane & 1) == 0,
                        pltpu.roll(t, t.shape[1] - 1, axis=1),
                        pltpu.roll(t, 1, axis=1))
    cos = cos_ref[...]
    sin = sin_ref[...]
    for h in range(t.shape[1] // RET_DK):
        sl = slice(h * RET_DK, (h + 1) * RET_DK)
        r = t[:, sl] * cos + swapped[:, sl] * sin
        if scale != 1.0:
            r = r * scale
        out_refs[0][:, sl] = r.astype(out_refs[0].dtype)


def _ep_forget(accs, extra_refs, out_refs, *, layer):
    logits_ref = extra_refs[0]
    rows = [logits_ref[r:r + 1, :] for r in range(N_HGRN)]
    mx = functools.reduce(jnp.maximum, rows)
    es = [jnp.exp(r - mx) for r in rows]
    den = functools.reduce(lambda a, b: a + b, es)
    ps = [e / den for e in es]
    lb = functools.reduce(lambda a, b: a + b, ps[:layer + 1]) - ps[0]
    fpre = accs[0]
    log_lb = jnp.log(jnp.maximum(lb, LB_FLOOR))
    log_sig = jnp.minimum(fpre, 0.0) - jnp.log1p(jnp.exp(-jnp.abs(fpre)))
    b = jnp.log1p(-lb) + log_sig
    log_f = jnp.maximum(log_lb, b) + jnp.log1p(jnp.exp(-jnp.abs(log_lb - b)))
    out_refs[0][...] = log_f
    out_refs[1][...] = (1.0 - lb) * _sigmoid(-fpre)


def _dense_ln_body(h_ref, w_ref, x_ref, g_ref, b_ref, of_ref, ob_ref, *, n_k):
    kk = pl.program_id(1)

    @pl.when(kk == 0)
    def _():
        of_ref[...] = ALPHA * x_ref[...]

    of_ref[...] += jnp.dot(h_ref[...], w_ref[...], preferred_element_type=F32)

    @pl.when(kk == n_k - 1)
    def _():
        y = of_ref[...]
        mu = jnp.mean(y, axis=-1, keepdims=True)
        d = y - mu
        var = jnp.mean(d * d, axis=-1, keepdims=True)
        r = d * lax.rsqrt(var + LN_EPS) * g_ref[...] + b_ref[...]
        of_ref[...] = r
        ob_ref[...] = r.astype(BF16)


def _ln_contraction_tile(k):
    cap = (8 * 1024 * 1024) // (2 * D_MODEL)
    return max(t for t in range(LANES, cap + 1, LANES) if k % t == 0)


def _dense_ln(h, w_bf16, layer, x_res, gain, bias, name):
    m, k = h.shape
    d = w_bf16.shape[2]
    tk = _ln_contraction_tile(k)
    assert m % TM_LN == 0 and k % tk == 0
    n_k = k // tk
    return pl.pallas_call(
        functools.partial(_dense_ln_body, n_k=n_k),
        grid=(m // TM_LN, n_k),
        in_specs=[pl.BlockSpec((TM_LN, tk), lambda i, kk: (i, kk)),
                  pl.BlockSpec((None, tk, d), lambda i, kk: (layer, kk, 0)),
                  pl.BlockSpec((TM_LN, d), lambda i, kk: (i, 0)),
                  pl.BlockSpec((1, d), lambda i, kk: (0, 0)),
                  pl.BlockSpec((1, d), lambda i, kk: (0, 0))],
        out_specs=[pl.BlockSpec((TM_LN, d), lambda i, kk: (i, 0)),
                   pl.BlockSpec((TM_LN, d), lambda i, kk: (i, 0))],
        out_shape=[jax.ShapeDtypeStruct((m, d), F32), jax.ShapeDtypeStruct((m, d), BF16)],
        compiler_params=_params("parallel", "arbitrary"), name=name,
    )(h, w_bf16, x_res, gain.reshape(1, d), bias.reshape(1, d))


def _group_norm_gate(o, gain, gate):
    mu = jnp.mean(o, axis=-1, keepdims=True)
    d = o - mu
    var = jnp.mean(d * d, axis=-1, keepdims=True)
    return gate * (d * lax.rsqrt(var + LN_EPS) * gain)


def _ret_step(q, k, v, s, dmat, qdec, kdec, cdec):
    a = lax.dot_general(q.astype(BF16), k.astype(BF16), (((1,), (1,)), ((), ())),
                        preferred_element_type=F32) * dmat
    o = jnp.dot(a.astype(BF16), v, preferred_element_type=F32)
    o = o + jnp.dot((q * qdec).astype(BF16), s.astype(BF16), preferred_element_type=F32)
    s_new = cdec * s + lax.dot_general((k * kdec).astype(BF16), v, (((0,), (0,)), ((), ())),
                                       preferred_element_type=F32)
    return o, s_new


def _ret_prompt_body(q_ref, k_ref, v_ref, g_ref, gain_ref, dmat_ref, qdec_ref, kdec_ref, cdec_ref,
                     o_ref, s_ref):
    @pl.when(pl.program_id(2) == 0)
    def _():
        s_ref[...] = jnp.zeros_like(s_ref)

    o, s_new = _ret_step(q_ref[...], k_ref[...], v_ref[...], s_ref[0, 0],
                         dmat_ref[0], qdec_ref[0], kdec_ref[0], cdec_ref[0])
    s_ref[0, 0] = s_new
    o_ref[...] = _group_norm_gate(o, gain_ref[...], g_ref[...]).astype(BF16)


def _ret_decay_tables(c):
    lg = jnp.log(1.0 - 2.0 ** (-5.0 - jnp.arange(RET_HEADS, dtype=F32)))
    idx = jnp.arange(c, dtype=F32)
    rel = idx[:, None] - idx[None, :]
    causal = rel >= 0
    dmat = jnp.where(causal[None], jnp.exp(jnp.where(causal, rel, 0.0)[None] * lg[:, None, None]), 0.0)
    qdec = jnp.exp((idx + 1)[None, :] * lg[:, None])
    kdec = jnp.exp((c - 1 - idx)[None, :] * lg[:, None])
    cdec = jnp.exp(c * lg)
    qdec = jnp.broadcast_to(qdec[:, :, None], (RET_HEADS, c, RET_DK))
    kdec = jnp.broadcast_to(kdec[:, :, None], (RET_HEADS, c, RET_DK))
    cdec = jnp.broadcast_to(cdec[:, None, None], (RET_HEADS, 1, RET_DV))
    return dmat, qdec, kdec, cdec


def _ret_prompt(q, k, v, g, gain):
    c = RET_C
    nc = SEQ // c
    dmat, qdec, kdec, cdec = _ret_decay_tables(c)
    row = lambda b, h, ci: (b * nc + ci, h)
    head = lambda b, h, ci: (h, 0, 0)
    return pl.pallas_call(
        _ret_prompt_body,
        grid=(BATCH, RET_HEADS, nc),
        in_specs=[pl.BlockSpec((c, RET_DK), row), pl.BlockSpec((c, RET_DK), row),
                  pl.BlockSpec((c, RET_DV), row), pl.BlockSpec((c, RET_DV), row),
                  pl.BlockSpec((1, RET_DV), lambda b, h, ci: (0, h)),
                  pl.BlockSpec((1, c, c), head), pl.BlockSpec((1, c, RET_DK), head),
                  pl.BlockSpec((1, c, RET_DK), head), pl.BlockSpec((1, 1, RET_DV), head)],
        out_specs=[pl.BlockSpec((c, RET_DV), row),
                   pl.BlockSpec((1, 1, RET_DK, RET_DV), lambda b, h, ci: (b, h, 0, 0))],
        out_shape=[jax.ShapeDtypeStruct((M_ROWS, RET_VD), BF16),
                   jax.ShapeDtypeStruct((BATCH, RET_HEADS, RET_DK, RET_DV), F32)],
        compiler_params=_params("parallel", "parallel", "arbitrary"), name="ret_prompt",
    )(q, k, v, g, gain.reshape(1, RET_VD), dmat, qdec, kdec, cdec)


def _ret_sample_body(q_ref, k_ref, v_ref, g_ref, gain_ref, dmat_ref, qdec_ref, kdec_ref, cdec_ref,
                     s0_ref, o_prev_ref, st_prev_ref, o_ref, s_ref):
    del o_prev_ref, st_prev_ref
    hb = pl.program_id(1)
    v_all = v_ref[...].astype(F32)
    outs = []
    for bi in range(SAMPLE_BB):
        rows = slice(bi * DEC_SEQ, (bi + 1) * DEC_SEQ)
        per_head = []
        for h in range(RET_S_HB):
            ks = slice(h * RET_DK, (h + 1) * RET_DK)
            vs = slice(h * RET_DV, (h + 1) * RET_DV)
            o, s_new = _ret_step(q_ref[rows, ks], k_ref[rows, ks], v_all[rows, vs].astype(BF16),
                                 s0_ref[0, bi, h], dmat_ref[h], qdec_ref[h], kdec_ref[h], cdec_ref[h])
            s_ref[0, bi, h] = s_new
            per_head.append(_group_norm_gate(o, gain_ref[:, vs], g_ref[rows, vs]))
        outs.append(jnp.concatenate(per_head, axis=1))
    del hb
    o_ref[...] = jnp.concatenate(outs, axis=0).astype(BF16)


def _ret_sample(q, k, v, g, gain, state_all, layer_j, o_prev, st_prev):
    c = DEC_SEQ
    dmat, qdec, kdec, cdec = _ret_decay_tables(c)
    rows_per = SAMPLE_BB * DEC_SEQ
    row0 = MP_ROWS // rows_per
    n_hb = RET_HEADS // RET_S_HB
    row = lambda bb, hb: (row0 + bb, hb)
    hsel = lambda bb, hb: (hb, 0, 0)
    st_block = (1, SAMPLE_BB, RET_S_HB, RET_DK, RET_DV)
    st_map = lambda bb, hb: (layer_j, bb, hb, 0, 0)
    in_specs = [pl.BlockSpec((rows_per, RET_S_HB * RET_DK), row),
                pl.BlockSpec((rows_per, RET_S_HB * RET_DK), row),
                pl.BlockSpec((rows_per, RET_S_HB * RET_DV), row),
                pl.BlockSpec((rows_per, RET_S_HB * RET_DV), row),
                pl.BlockSpec((1, RET_S_HB * RET_DV), lambda bb, hb: (0, hb)),
                pl.BlockSpec((RET_S_HB, c, c), hsel),
                pl.BlockSpec((RET_S_HB, c, RET_DK), hsel),
                pl.BlockSpec((RET_S_HB, c, RET_DK), hsel),
                pl.BlockSpec((RET_S_HB, 1, RET_DV), hsel),
                pl.BlockSpec(st_block, st_map),
                pl.BlockSpec(memory_space=pl.ANY)]
    operands = [q, k, v, g, gain.reshape(1, RET_VD), dmat, qdec, kdec, cdec, state_all, o_prev]
    aliases = {10: 0}
    if st_prev is not None:
        in_specs.append(pl.BlockSpec(memory_space=pl.ANY))
        operands.append(st_prev)
        aliases[11] = 1
        body = _ret_sample_body
    else:
        body = lambda *refs: _ret_sample_body(*refs[:11], None, *refs[11:])
    return pl.pallas_call(
        body,
        grid=(DEC_BATCH // SAMPLE_BB, n_hb),
        in_specs=in_specs,
        out_specs=[pl.BlockSpec((rows_per, RET_S_HB * RET_DV), row), pl.BlockSpec(st_block, st_map)],
        out_shape=[jax.ShapeDtypeStruct((M_ROWS, RET_VD), BF16),
                   jax.ShapeDtypeStruct((N_RET, DEC_BATCH, RET_HEADS, RET_DK, RET_DV), F32)],
        input_output_aliases=aliases,
        compiler_params=_params("parallel", "parallel"), name="ret_sample",
    )(*operands)


def _rms_norm_gate(o, gain, gate):
    return o * lax.rsqrt(jnp.mean(o * o, axis=-1, keepdims=True) + LN_EPS) * gain * gate


def _diag_tiles(g2_blk, q_blk, g2_row, k_row):
    n = g2_blk.shape[0]
    return [jnp.exp2(g2_blk - g2_row(j)) * q_blk * k_row(j) for j in range(n)]


def _select_diag(sums, n, lane_base):
    lanes = sums.shape[1]
    rel = lax.broadcasted_iota(jnp.int32, (n, lanes), 1) - lane_base
    row = lax.broadcasted_iota(jnp.int32, (n, lanes), 0)
    acc = jnp.zeros((n, lanes), F32)
    for j in range(n):
        acc = jnp.where(rel == j, sums[j * n:(j + 1) * n, :], acc)
    return jnp.where(row >= rel, acc, 0.0)


def _gla_prompt_body(q_ref, k_ref, lf_ref, v_ref, g_ref, gain_ref, ltri_ref, o_ref, st_ref, s_scr, g_scr):
    ci = pl.program_id(2)
    c = GLA_C
    n = GLA_SUB

    @pl.when(ci == 0)
    def _():
        s_scr[...] = jnp.zeros_like(s_scr)

    width = q_ref.shape[1]
    heads = [slice(hh * HG_DK, (hh + 1) * HG_DK) for hh in range(GLA_HB)]
    ones_b = jnp.ones((HG_DK, c), BF16)
    row = lax.broadcasted_iota(jnp.int32, (c, width), 0)
    ri = lax.broadcasted_iota(jnp.int32, (c, c), 0)
    cj = lax.broadcasted_iota(jnp.int32, (c, c), 1)

    g2 = LOG2_E * jnp.dot(ltri_ref[...], lf_ref[...], precision=lax.Precision.HIGHEST,
                          preferred_element_type=F32)
    g_scr[...] = g2
    q = q_ref[...]
    k = k_ref[...]

    a = [jnp.zeros((c, c), F32) for _ in heads]
    s = c // 2
    while s >= n:
        ref_rows = [jnp.broadcast_to(g_scr[m + s - 1:m + s, :], (2 * s, width)) for m in range(0, c, 2 * s)]
        ref = ref_rows[0] if len(ref_rows) == 1 else jnp.concatenate(ref_rows, axis=0)
        e = jnp.exp2(-jnp.abs(g2 - ref))
        second = (row & s) != 0
        qt = jnp.where(second, q * e, 0.0).astype(BF16)
        kt = jnp.where(second, 0.0, k * e).astype(BF16)
        for hh, sl in enumerate(heads):
            p = lax.dot_general(qt[:, sl], kt[:, sl], (((1,), (1,)), ((), ())), preferred_element_type=F32)
            if 2 * s < c:
                sh = int(math.log2(2 * s))
                p = jnp.where((ri >> sh) == (cj >> sh), p, 0.0)
            a[hh] = a[hh] + p
        s //= 2

    tiles = []
    for sl in heads:
        for m in range(0, c, n):
            tiles += _diag_tiles(g2[m:m + n, sl], q[m:m + n, sl],
                                 lambda j, m=m, sl=sl: g_scr[m + j:m + j + 1, sl],
                                 lambda j, m=m, sl=sl: k_ref[m + j:m + j + 1, sl])
    sums = jnp.dot(jnp.concatenate(tiles, axis=0).astype(BF16), ones_b, preferred_element_type=F32)

    qe = (q * jnp.exp2(g2)).astype(BF16)
    g2_last = g_scr[c - 1:c, :]
    kd = (k * jnp.exp2(g2_last - g2)).astype(BF16)
    decay = jnp.exp2(g2_last)
    outs = []
    for hh, sl in enumerate(heads):
        base = hh * c * n
        diag = [_select_diag(sums[base + m * n:base + (m + n) * n, :], n, m) for m in range(0, c, n)]
        a_h = (a[hh] + jnp.concatenate(diag, axis=0)).astype(BF16)
        v = v_ref[:, sl]
        s_t = s_scr[hh]
        o = jnp.dot(a_h, v, preferred_element_type=F32)
        o = o + lax.dot_general(qe[:, sl], s_t.astype(BF16), (((1,), (1,)), ((), ())),
                                preferred_element_type=F32)
        s_scr[hh] = decay[:, sl] * s_t + lax.dot_general(
            v, kd[:, sl], (((0,), (0,)), ((), ())), preferred_element_type=F32)
        outs.append(_rms_norm_gate(o, gain_ref[...], g_ref[:, sl]))

    o_ref[...] = jnp.concatenate(outs, axis=1).astype(BF16)

    @pl.when(ci == pl.num_programs(2) - 1)
    def _():
        for hh in range(GLA_HB):
            st_ref[0, hh] = s_scr[hh].T


def _gla_prompt(q, k, lf, v, g, gain):
    c = GLA_C
    nc = SEQ // c
    w = GLA_HB * HG_DK
    ltri = (jnp.arange(c)[:, None] >= jnp.arange(c)[None, :]).astype(F32)
    row = lambda b, hb, ci: (b * nc + ci, hb)
    return pl.pallas_call(
        _gla_prompt_body,
        grid=(BATCH, HG_HEADS // GLA_HB, nc),
        in_specs=[pl.BlockSpec((c, w), row), pl.BlockSpec((c, w), row), pl.BlockSpec((c, w), row),
                  pl.BlockSpec((c, w), row), pl.BlockSpec((c, w), row),
                  pl.BlockSpec((1, HG_DV), lambda b, hb, ci: (0, 0)),
                  pl.BlockSpec((c, c), lambda b, hb, ci: (0, 0))],
        out_specs=[pl.BlockSpec((c, w), row),
                   pl.BlockSpec((1, GLA_HB, HG_DK, HG_DV), lambda b, hb, ci: (b, hb, 0, 0))],
        out_shape=[jax.ShapeDtypeStruct((M_ROWS, D_MODEL), BF16),
                   jax.ShapeDtypeStruct((BATCH, HG_HEADS, HG_DK, HG_DV), F32)],
        scratch_shapes=[pltpu.VMEM((GLA_HB, HG_DV, HG_DK), F32), pltpu.VMEM((c, w), F32)],
        compiler_params=_params("parallel", "parallel", "arbitrary"), name="gla_prompt",
    )(q, k, lf, v, g, gain.reshape(1, HG_DV), ltri)


def _gla_sample_body(q_ref, k_ref, lf_ref, v_ref, g_ref, gain_ref, s0_ref, o_prev_ref, st_prev_ref,
                     o_ref, s_ref, g_scr):
    del o_prev_ref, st_prev_ref
    n = DEC_SEQ
    rows_per = SAMPLE_BB * n
    width = q_ref.shape[1]
    rix = lax.broadcasted_iota(jnp.int32, (rows_per, width), 0) & (n - 1)
    g2_all = lf_ref[...]
    sh = 1
    while sh < n:
        g2_all = g2_all + jnp.where(rix >= sh, pltpu.roll(g2_all, sh, axis=0), 0.0)
        sh *= 2
    g2_all = LOG2_E * g2_all
    g_scr[...] = g2_all
    q_all = q_ref[...]
    g2_last = jnp.concatenate([jnp.broadcast_to(g_scr[(bi + 1) * n - 1:(bi + 1) * n, :], (n, width))
                               for bi in range(SAMPLE_BB)], axis=0)
    qe_all = q_all * jnp.exp2(g2_all)
    kd_all = k_ref[...] * jnp.exp2(g2_last - g2_all)
    decay_all = jnp.exp2(g2_last)
    v_all = v_ref[...].astype(F32)
    ones_b = jnp.ones((HG_DK, HG_DK), BF16)
    zpad = jnp.zeros((HG_DK - n, HG_DV), F32)
    units = [(bi, h) for bi in range(SAMPLE_BB) for h in range(HG_HEADS)]
    span = lambda bi, h: (slice(bi * n, (bi + 1) * n), slice(h * HG_DK, (h + 1) * HG_DK))

    tiles = []
    for bi, h in units:
        rows, sl = span(bi, h)
        tiles += _diag_tiles(g2_all[rows, sl], q_all[rows, sl],
                             lambda j, bi=bi, sl=sl: g_scr[bi * n + j:bi * n + j + 1, sl],
                             lambda j, bi=bi, sl=sl: k_ref[bi * n + j:bi * n + j + 1, sl])
    sums = jnp.dot(jnp.concatenate(tiles, axis=0).astype(BF16), ones_b, preferred_element_type=F32)

    v_pads, o_units = [], []
    for u, (bi, h) in enumerate(units):
        rows, sl = span(bi, h)
        a = _select_diag(sums[u * n * n:(u + 1) * n * n, :], n, 0)
        v_pad = jnp.concatenate([v_all[rows, sl], zpad], axis=0).astype(BF16)
        v_pads.append(v_pad)
        o = jnp.dot(a.astype(BF16), v_pad, preferred_element_type=F32)
        o_units.append(o + jnp.dot(qe_all[rows, sl].astype(BF16), s0_ref[0, bi, h].astype(BF16),
                                   preferred_element_type=F32))
    for u, (bi, h) in enumerate(units):
        rows, sl = span(bi, h)
        kd_pad = jnp.concatenate([kd_all[rows, sl], zpad], axis=0).astype(BF16)
        decay_col = jnp.broadcast_to(decay_all[bi * n:bi * n + 1, sl], (HG_DV, HG_DK)).T
        s_ref[0, bi, h] = decay_col * s0_ref[0, bi, h] + lax.dot_general(
            kd_pad, v_pads[u], (((0,), (0,)), ((), ())), preferred_element_type=F32)
    outs = []
    for bi in range(SAMPLE_BB):
        per_head = [_rms_norm_gate(o_units[bi * HG_HEADS + h], gain_ref[...], g_ref[span(bi, h)])
                    for h in range(HG_HEADS)]
        outs.append(jnp.concatenate(per_head, axis=1))
    o_ref[...] = jnp.concatenate(outs, axis=0).astype(BF16)


def _gla_sample(q, k, lf, v, g, gain, state_all, layer_j, o_prev, st_prev):
    rows_per = SAMPLE_BB * DEC_SEQ
    row0 = MP_ROWS // rows_per
    row = lambda bb: (row0 + bb, 0)
    st_block = (1, SAMPLE_BB, HG_HEADS, HG_DK, HG_DV)
    st_map = lambda bb: (layer_j, bb, 0, 0, 0)
    in_specs = [pl.BlockSpec((rows_per, D_MODEL), row)] * 5 + [
        pl.BlockSpec((1, HG_DV), lambda bb: (0, 0)),
        pl.BlockSpec(st_block, st_map),
        pl.BlockSpec(memory_space=pl.ANY)]
    operands = [q, k, lf, v, g, gain.reshape(1, HG_DV), state_all, o_prev]
    aliases = {7: 0}
    if st_prev is not None:
        in_specs.append(pl.BlockSpec(memory_space=pl.ANY))
        operands.append(st_prev)
        aliases[8] = 1
        body = _gla_sample_body
    else:
        body = lambda *refs: _gla_sample_body(*refs[:8], None, *refs[8:])
    return pl.pallas_call(
        body,
        grid=(DEC_BATCH // SAMPLE_BB,),
        in_specs=in_specs,
        out_specs=[pl.BlockSpec((rows_per, D_MODEL), row), pl.BlockSpec(st_block, st_map)],
        out_shape=[jax.ShapeDtypeStruct((M_ROWS, D_MODEL), BF16),
                   jax.ShapeDtypeStruct((N_HGRN, DEC_BATCH, HG_HEADS, HG_DK, HG_DV), F32)],
        input_output_aliases=aliases,
        scratch_shapes=[pltpu.VMEM((rows_per, D_MODEL), F32)],
        compiler_params=_params("parallel"), name="gla_sample",
    )(*operands)


def _rope_tables():
    inv = 1.0 / (ROPE_BASE ** jnp.linspace(0.0, 1.0, RET_DK // 2, dtype=F32))
    pos = jnp.concatenate([jnp.tile(0 + jnp.arange(SEQ, dtype=F32), BATCH),
                           jnp.tile(PAST_LEN + jnp.arange(DEC_SEQ, dtype=F32), DEC_BATCH)])
    ang = pos[:, None] * inv[None, :]
    cos = jnp.repeat(jnp.cos(ang), 2, axis=-1)
    sin = jnp.repeat(jnp.sin(ang), 2, axis=-1)
    sign = jnp.tile(jnp.array([-1.0, 1.0], F32), RET_DK // 2)
    return cos, sin * sign[None, :]


def _retention_layer(xb, w_in, gn_gain, state_all, layer_j, st_prev, rope):
    cos, sin = rope
    rope_specs = [pl.BlockSpec((TM, RET_DK), lambda j, i: (i, 0))] * 2
    (q,) = _dense(xb, w_in, layer_j, [0], RET_KD, functools.partial(_ep_rotary, scale=1.0), [F32],
                  extras=(cos, sin), extra_specs=rope_specs, tn=TN_WIDE, name="ret_q")
    (k,) = _dense(xb, w_in, layer_j, [RET_KD], RET_KD,
                  functools.partial(_ep_rotary, scale=RET_DK ** -0.5), [F32],
                  extras=(cos, sin), extra_specs=rope_specs, tn=TN_WIDE, name="ret_k")
    (v,) = _dense(xb, w_in, layer_j, [2 * RET_KD], RET_VD, _ep_plain, [BF16], tn=TN_WIDE, name="ret_v")
    (g,) = _dense(xb, w_in, layer_j, [2 * RET_KD + RET_VD], RET_VD, _ep_silu, [F32], tn=TN_WIDE,
                  name="ret_g")
    o, st_p = _ret_prompt(q, k, v, g, gn_gain)
    o, st_s = _ret_sample(q, k, v, g, gn_gain, state_all, layer_j, o, st_prev)
    return o, st_p, st_s


def _hgrn_layer(xb, w_in, lb_logits, norm_gain, state_all, layer_j, st_prev):
    d = D_MODEL
    logit_spec = [pl.BlockSpec((N_HGRN, TN), lambda j, i: (0, j))]
    (q,) = _dense(xb, w_in, layer_j, [0], d, functools.partial(_ep_silu_scaled, scale=HG_DK ** -0.5),
                  [F32], tn=TN_WIDE, name="hg_q")
    lf, k = _dense(xb, w_in, layer_j, [d], d, functools.partial(_ep_forget, layer=layer_j), [F32, F32],
                   extras=(lb_logits,), extra_specs=logit_spec, name="hg_f")
    (v,) = _dense(xb, w_in, layer_j, [2 * d], d, _ep_plain, [BF16], tn=TN_WIDE, name="hg_v")
    (g,) = _dense(xb, w_in, layer_j, [3 * d], d, _ep_silu, [F32], tn=TN_WIDE, name="hg_g")
    o, st_p = _gla_prompt(q, k, lf, v, g, norm_gain)
    o, st_s = _gla_sample(q, k, lf, v, g, norm_gain, state_all, layer_j, o, st_prev)
    return o, st_p, st_s


def kernel(x_prompt, x_sample, state_ret, state_hgrn, ret_w_in, ret_gn_gain, ret_w_out,
           hgrn_w_in, hgrn_lb_logits, hgrn_norm_gain, hgrn_w_out,
           ln_mix_g, ln_mix_b, ffn_w_in, ffn_w_out, ln_ffn_g, ln_ffn_b):
    assert x_prompt.shape == (BATCH, SEQ, D_MODEL) and x_sample.shape == (DEC_BATCH, DEC_SEQ, D_MODEL)
    assert state_ret.shape == (N_RET, DEC_BATCH, RET_HEADS, RET_DK, RET_DV)
    assert state_hgrn.shape == (N_HGRN, DEC_BATCH, HG_HEADS, HG_DK, HG_DV)
    assert ffn_w_in.shape == (DEPTH, D_MODEL, 2 * D_FF)

    x = jnp.concatenate([x_prompt.reshape(MP_ROWS, D_MODEL), x_sample.reshape(MS_ROWS, D_MODEL)], axis=0)
    xb = x.astype(BF16)
    rope = _rope_tables()
    ret_w_out_b = ret_w_out.astype(BF16)
    hgrn_w_out_b = hgrn_w_out.astype(BF16)
    ffn_w_out_b = ffn_w_out.astype(BF16)

    ret_p, hg_p = [], []
    ret_s = None
    hg_s = None
    for layer in range(DEPTH):
        j = layer // 2
        if layer % 2 == 0:
            o, st_p, ret_s = _retention_layer(xb, ret_w_in, ret_gn_gain[j], state_ret, j, ret_s, rope)
            ret_p.append(st_p)
            w_out = ret_w_out_b
        else:
            o, st_p, hg_s = _hgrn_layer(xb, hgrn_w_in, hgrn_lb_logits, hgrn_norm_gain[j],
                                        state_hgrn, j, hg_s)
            hg_p.append(st_p)
            w_out = hgrn_w_out_b
        x, xb = _dense_ln(o, w_out, j, x, ln_mix_g[layer], ln_mix_b[layer], name="mix_out")
        (h,) = _dense(xb, ffn_w_in, layer, [0, D_FF], D_FF, _ep_swiglu, [BF16], name="ffn_in")
        x, xb = _dense_ln(h, ffn_w_out_b, layer, x, ln_ffn_g[layer], ln_ffn_b[layer], name="ffn_out")

    return (x[:MP_ROWS].reshape(BATCH, SEQ, D_MODEL),
            x[MP_ROWS:].reshape(DEC_BATCH, DEC_SEQ, D_MODEL),
            jnp.stack(ret_p), ret_s, jnp.stack(hg_p), hg_s)
```

```python
import functools
import math

import jax
import jax.numpy as jnp
from jax import lax
from jax.experimental import pallas as pl
from jax.experimental.pallas import tpu as pltpu

F32 = jnp.float32
BF16 = jnp.bfloat16

D_MODEL = 2048
BATCH = 4
SEQ = 2048
DEPTH = 4
DEC_BATCH = 128
DEC_SEQ = 8
PAST_LEN = 16384
N_RET = 2
N_HGRN = 2
RET_HEADS = 8
RET_DK = 256
RET_DV = 512
RET_KD = RET_HEADS * RET_DK
RET_VD = RET_HEADS * RET_DV
HG_HEADS = 16
HG_DK = 128
HG_DV = 128
D_FF = 5632
ROPE_BASE = 10000.0
ALPHA = (2 * DEPTH) ** 0.25
LN_EPS = 1e-5
LB_FLOOR = 1e-30
LOG2_E = 1.4426950408889634

MP_ROWS = BATCH * SEQ
MS_ROWS = DEC_BATCH * DEC_SEQ
M_ROWS = MP_ROWS + MS_ROWS

VMEM_LIMIT_BYTES = 56 * 1024 * 1024

LANES = 128

TM = 1024
TN_WIDE = 1024
TN = 512
TM_LN = 1024
LN_ROWS = 256

RET_C = 256
RET_HB = 4
GLA_C = 128
GLA_SUB = 8
GLA_HB = 8
SAMPLE_BB = 2
RET_S_HB = 4


def _params(*sem):
    return pltpu.CompilerParams(dimension_semantics=sem, vmem_limit_bytes=VMEM_LIMIT_BYTES)


def _sigmoid(x):
    return 1.0 / (1.0 + jnp.exp(-x))


def _silu(x):
    return x * _sigmoid(x)


def _dense_body(*refs, n_w, n_extra, n_out, epilogue):
    x_ref = refs[0]
    w_refs = refs[1:1 + n_w]
    extra_refs = refs[1 + n_w:1 + n_w + n_extra]
    out_refs = refs[1 + n_w + n_extra:1 + n_w + n_extra + n_out]
    wb_refs = refs[1 + n_w + n_extra + n_out:]

    @pl.when(pl.program_id(1) == 0)
    def _():
        for w_ref, wb_ref in zip(w_refs, wb_refs):
            wb_ref[...] = w_ref[...].astype(BF16)

    x = x_ref[...]
    accs = [jnp.dot(x, wb_ref[...], preferred_element_type=F32) for wb_ref in wb_refs]
    epilogue(accs, extra_refs, out_refs)


def _dense(x, w, layer, col_offs, n_cols, epilogue, out_dtypes, extras=(), extra_specs=(), tn=TN,
           name="dense"):
    m, k = x.shape
    assert m % TM == 0 and n_cols % tn == 0 and all(c % tn == 0 for c in col_offs)
    grid = (n_cols // tn, m // TM)
    in_specs = [pl.BlockSpec((TM, k), lambda j, i: (i, 0))]
    for off in col_offs:
        in_specs.append(pl.BlockSpec((None, k, tn),
                                     functools.partial(lambda j, i, o: (layer, 0, j + o), o=off // tn)))
    in_specs += list(extra_specs)
    out_specs = [pl.BlockSpec((TM, tn), lambda j, i: (i, j)) for _ in out_dtypes]
    out_shape = [jax.ShapeDtypeStruct((m, n_cols), dt) for dt in out_dtypes]
    body = functools.partial(_dense_body, n_w=len(col_offs), n_extra=len(extras),
                             n_out=len(out_dtypes), epilogue=epilogue)
    return pl.pallas_call(
        body, grid=grid, in_specs=in_specs, out_specs=out_specs, out_shape=out_shape,
        scratch_shapes=[pltpu.VMEM((k, tn), BF16) for _ in col_offs],
        compiler_params=_params("arbitrary", "arbitrary"), name=name,
    )(x, *([w] * len(col_offs)), *extras)


def _ep_plain(accs, extra_refs, out_refs):
    out_refs[0][...] = accs[0].astype(out_refs[0].dtype)


def _ep_silu(accs, extra_refs, out_refs):
    out_refs[0][...] = _silu(accs[0]).astype(out_refs[0].dtype)


def _ep_silu_scaled(accs, extra_refs, out_refs, *, scale):
    out_refs[0][...] = (_silu(accs[0]) * scale).astype(out_refs[0].dtype)


def _ep_swiglu(accs, extra_refs, out_refs):
    out_refs[0][...] = (_silu(accs[0]) * accs[1]).astype(out_refs[0].dtype)


def _ep_rotary(accs, extra_refs, out_refs, *, scale):
    cos_ref, sin_ref = extra_refs
    t = accs[0]
    lane = lax.broadcasted_iota(jnp.int32, t.shape, 1)
    swapped = jnp.where((lane & 1) == 0,
                        pltpu.roll(t, t.shape[1] - 1, axis=1),
                        pltpu.roll(t, 1, axis=1))
    cos = cos_ref[...]
    sin = sin_ref[...]
    for h in range(t.shape[1] // RET_DK):
        sl = slice(h * RET_DK, (h + 1) * RET_DK)
        r = t[:, sl] * cos + swapped[:, sl] * sin
        if scale != 1.0:
            r = r * scale
        out_refs[0][:, sl] = r.astype(out_refs[0].dtype)


def _ep_forget(accs, extra_refs, out_refs, *, layer):
    logits_ref = extra_refs[0]
    rows = [logits_ref[r:r + 1, :] for r in range(N_HGRN)]
    mx = functools.reduce(jnp.maximum, rows)
    es = [jnp.exp(r - mx) for r in rows]
    den = functools.reduce(lambda a, b: a + b, es)
    ps = [e / den for e in es]
    lb = functools.reduce(lambda a, b: a + b, ps[:layer + 1]) - ps[0]
    fpre = accs[0]
    t = jnp.exp(-jnp.abs(fpre))
    r = 1.0 / (1.0 + t)
    tr = t * r
    pos = fpre >= 0.0
    one_m_lb = 1.0 - lb
    out_refs[0][...] = jnp.log(jnp.maximum(lb, LB_FLOOR) + one_m_lb * jnp.where(pos, r, tr))
    out_refs[1][...] = one_m_lb * jnp.where(pos, tr, r)


def _dense_ln_body(h_ref, w_ref, x_hbm, g_ref, b_ref, of_ref, ob_ref, x_buf, x_sem, *, n_k):
    i = pl.program_id(0)
    kk = pl.program_id(1)
    tm = of_ref.shape[0]
    x_copy = pltpu.make_async_copy(x_hbm.at[pl.ds(pl.multiple_of(i * tm, tm), tm), :], x_buf, x_sem)

    @pl.when(kk == 0)
    def _():
        x_copy.start()
        of_ref[...] = jnp.zeros_like(of_ref)

    of_ref[...] += jnp.dot(h_ref[...], w_ref[...], preferred_element_type=F32)

    @pl.when(kk == n_k - 1)
    def _():
        x_copy.wait()
        for r0 in range(0, tm, LN_ROWS):
            rows = slice(r0, r0 + LN_ROWS)
            y = of_ref[rows, :] + ALPHA * x_buf[rows, :]
            mu = jnp.mean(y, axis=-1, keepdims=True)
            d = y - mu
            var = jnp.mean(d * d, axis=-1, keepdims=True)
            r = d * lax.rsqrt(var + LN_EPS) * g_ref[...] + b_ref[...]
            of_ref[rows, :] = r
            ob_ref[rows, :] = r.astype(BF16)


def _ln_contraction_tile(k):
    cap = (6 * 1024 * 1024) // (2 * D_MODEL)
    return max(t for t in range(LANES, cap + 1, LANES) if k % t == 0)


def _dense_ln(h, w_bf16, layer, x_res, gain, bias, name):
    m, k = h.shape
    d = w_bf16.shape[2]
    tk = _ln_contraction_tile(k)
    assert m % TM_LN == 0 and k % tk == 0
    n_k = k // tk
    return pl.pallas_call(
        functools.partial(_dense_ln_body, n_k=n_k),
        grid=(m // TM_LN, n_k),
        in_specs=[pl.BlockSpec((TM_LN, tk), lambda i, kk: (i, kk)),
                  pl.BlockSpec((None, tk, d), lambda i, kk: (layer, kk, 0)),
                  pl.BlockSpec(memory_space=pl.ANY),
                  pl.BlockSpec((1, d), lambda i, kk: (0, 0)),
                  pl.BlockSpec((1, d), lambda i, kk: (0, 0))],
        out_specs=[pl.BlockSpec((TM_LN, d), lambda i, kk: (i, 0)),
                   pl.BlockSpec((TM_LN, d), lambda i, kk: (i, 0))],
        out_shape=[jax.ShapeDtypeStruct((m, d), F32), jax.ShapeDtypeStruct((m, d), BF16)],
        scratch_shapes=[pltpu.VMEM((TM_LN, d), F32), pltpu.SemaphoreType.DMA(())],
        compiler_params=_params("arbitrary", "arbitrary"), name=name,
    )(h, w_bf16, x_res, gain.reshape(1, d), bias.reshape(1, d))


def _group_norm_gate(o, gain, gate):
    mu = jnp.mean(o, axis=-1, keepdims=True)
    d = o - mu
    var = jnp.mean(d * d, axis=-1, keepdims=True)
    return gate * (d * lax.rsqrt(var + LN_EPS) * gain)


def _ret_step(q, k, v, s, dmat, qdec, kdec, cdec):
    a = lax.dot_general(q.astype(BF16), k.astype(BF16), (((1,), (1,)), ((), ())),
                        preferred_element_type=F32) * dmat
    o = jnp.dot(a.astype(BF16), v, preferred_element_type=F32)
    o = o + jnp.dot((q * qdec).astype(BF16), s.astype(BF16), preferred_element_type=F32)
    s_new = cdec * s + lax.dot_general((k * kdec).astype(BF16), v, (((0,), (0,)), ((), ())),
                                       preferred_element_type=F32)
    return o, s_new


def _ret_prompt_body(q_ref, k_ref, v_ref, g_ref, gain_ref, dmat_ref, qdec_ref, kdec_ref, cdec_ref,
                     o_ref, s_ref):
    @pl.when(pl.program_id(2) == 0)
    def _():
        s_ref[...] = jnp.zeros_like(s_ref)

    heads = range(RET_HB)
    ks = [slice(h * RET_DK, (h + 1) * RET_DK) for h in heads]
    vs = [slice(h * RET_DV, (h + 1) * RET_DV) for h in heads]
    q = q_ref[...]
    k = k_ref[...]
    qb = q.astype(BF16)
    kb = k.astype(BF16)
    qd = (q * qdec_ref[...]).astype(BF16)
    kd = (k * kdec_ref[...]).astype(BF16)
    a = [(lax.dot_general(qb[:, ks[h]], kb[:, ks[h]], (((1,), (1,)), ((), ())),
                          preferred_element_type=F32) * dmat_ref[h]).astype(BF16) for h in heads]
    s_old = [s_ref[0, h] for h in heads]
    o = [jnp.dot(a[h], v_ref[:, vs[h]], preferred_element_type=F32)
         + jnp.dot(qd[:, ks[h]], s_old[h].astype(BF16), preferred_element_type=F32) for h in heads]
    for h in heads:
        s_ref[0, h] = cdec_ref[:, vs[h]] * s_old[h] + lax.dot_general(
            kd[:, ks[h]], v_ref[:, vs[h]], (((0,), (0,)), ((), ())), preferred_element_type=F32)
    o_ref[...] = jnp.concatenate(
        [_group_norm_gate(o[h], gain_ref[:, vs[h]], g_ref[:, vs[h]]) for h in heads], axis=1).astype(BF16)


def _ret_decay_tables(c):
    lg = jnp.log(1.0 - 2.0 ** (-5.0 - jnp.arange(RET_HEADS, dtype=F32)))
    idx = jnp.arange(c, dtype=F32)
    rel = idx[:, None] - idx[None, :]
    causal = rel >= 0
    dmat = jnp.where(causal[None], jnp.exp(jnp.where(causal, rel, 0.0)[None] * lg[:, None, None]), 0.0)
    qdec = jnp.exp((idx + 1)[None, :] * lg[:, None])
    kdec = jnp.exp((c - 1 - idx)[None, :] * lg[:, None])
    cdec = jnp.exp(c * lg)
    qdec = jnp.broadcast_to(qdec[:, :, None], (RET_HEADS, c, RET_DK))
    kdec = jnp.broadcast_to(kdec[:, :, None], (RET_HEADS, c, RET_DK))
    cdec = jnp.broadcast_to(cdec[:, None, None], (RET_HEADS, 1, RET_DV))
    return dmat, qdec, kdec, cdec


def _ret_prompt(q, k, v, g, gain):
    c = RET_C
    nc = SEQ // c
    dmat, qdec, kdec, cdec = _ret_decay_tables(c)
    qdec = jnp.transpose(qdec, (1, 0, 2)).reshape(c, RET_KD)
    kdec = jnp.transpose(kdec, (1, 0, 2)).reshape(c, RET_KD)
    cdec = jnp.transpose(cdec, (1, 0, 2)).reshape(1, RET_VD)
    kw = RET_HB * RET_DK
    vw = RET_HB * RET_DV
    row = lambda b, h, ci: (b * nc + ci, h)
    lane = lambda b, h, ci: (0, h)
    return pl.pallas_call(
        _ret_prompt_body,
        grid=(BATCH, RET_HEADS // RET_HB, nc),
        in_specs=[pl.BlockSpec((c, kw), row), pl.BlockSpec((c, kw), row),
                  pl.BlockSpec((c, vw), row), pl.BlockSpec((c, vw), row),
                  pl.BlockSpec((1, vw), lane),
                  pl.BlockSpec((RET_HB, c, c), lambda b, h, ci: (h, 0, 0)), pl.BlockSpec((c, kw), lane),
                  pl.BlockSpec((c, kw), lane), pl.BlockSpec((1, vw), lane)],
        out_specs=[pl.BlockSpec((c, vw), row),
                   pl.BlockSpec((1, RET_HB, RET_DK, RET_DV), lambda b, h, ci: (b, h, 0, 0))],
        out_shape=[jax.ShapeDtypeStruct((M_ROWS, RET_VD), BF16),
                   jax.ShapeDtypeStruct((BATCH, RET_HEADS, RET_DK, RET_DV), F32)],
        compiler_params=_params("parallel", "parallel", "arbitrary"), name="ret_prompt",
    )(q, k, v, g, gain.reshape(1, RET_VD), dmat, qdec, kdec, cdec)


def _ret_sample_body(q_ref, k_ref, v_ref, g_ref, gain_ref, dmat_ref, qdec_ref, kdec_ref, cdec_ref,
                     s0_ref, o_prev_ref, st_prev_ref, o_ref, s_ref):
    del o_prev_ref, st_prev_ref
    hb = pl.program_id(1)
    v_all = v_ref[...].astype(F32)
    outs = []
    for bi in range(SAMPLE_BB):
        rows = slice(bi * DEC_SEQ, (bi + 1) * DEC_SEQ)
        per_head = []
        for h in range(RET_S_HB):
            ks = slice(h * RET_DK, (h + 1) * RET_DK)
            vs = slice(h * RET_DV, (h + 1) * RET_DV)
            o, s_new = _ret_step(q_ref[rows, ks], k_ref[rows, ks], v_all[rows, vs].astype(BF16),
                                 s0_ref[0, bi, h], dmat_ref[h], qdec_ref[h], kdec_ref[h], cdec_ref[h])
            s_ref[0, bi, h] = s_new
            per_head.append(_group_norm_gate(o, gain_ref[:, vs], g_ref[rows, vs]))
        outs.append(jnp.concatenate(per_head, axis=1))
    del hb
    o_ref[...] = jnp.concatenate(outs, axis=0).astype(BF16)


def _ret_sample(q, k, v, g, gain, state_all, layer_j, o_prev, st_prev):
    c = DEC_SEQ
    dmat, qdec, kdec, cdec = _ret_decay_tables(c)
    rows_per = SAMPLE_BB * DEC_SEQ
    row0 = MP_ROWS // rows_per
    n_hb = RET_HEADS // RET_S_HB
    row = lambda bb, hb: (row0 + bb, hb)
    hsel = lambda bb, hb: (hb, 0, 0)
    st_block = (1, SAMPLE_BB, RET_S_HB, RET_DK, RET_DV)
    st_map = lambda bb, hb: (layer_j, bb, hb, 0, 0)
    in_specs = [pl.BlockSpec((rows_per, RET_S_HB * RET_DK), row),
                pl.BlockSpec((rows_per, RET_S_HB * RET_DK), row),
                pl.BlockSpec((rows_per, RET_S_HB * RET_DV), row),
                pl.BlockSpec((rows_per, RET_S_HB * RET_DV), row),
                pl.BlockSpec((1, RET_S_HB * RET_DV), lambda bb, hb: (0, hb)),
                pl.BlockSpec((RET_S_HB, c, c), hsel),
                pl.BlockSpec((RET_S_HB, c, RET_DK), hsel),
                pl.BlockSpec((RET_S_HB, c, RET_DK), hsel),
                pl.BlockSpec((RET_S_HB, 1, RET_DV), hsel),
                pl.BlockSpec(st_block, st_map),
                pl.BlockSpec(memory_space=pl.ANY)]
    operands = [q, k, v, g, gain.reshape(1, RET_VD), dmat, qdec, kdec, cdec, state_all, o_prev]
    aliases = {10: 0}
    if st_prev is not None:
        in_specs.append(pl.BlockSpec(memory_space=pl.ANY))
        operands.append(st_prev)
        aliases[11] = 1
        body = _ret_sample_body
    else:
        body = lambda *refs: _ret_sample_body(*refs[:11], None, *refs[11:])
    return pl.pallas_call(
        body,
        grid=(DEC_BATCH // SAMPLE_BB, n_hb),
        in_specs=in_specs,
        out_specs=[pl.BlockSpec((rows_per, RET_S_HB * RET_DV), row), pl.BlockSpec(st_block, st_map)],
        out_shape=[jax.ShapeDtypeStruct((M_ROWS, RET_VD), BF16),
                   jax.ShapeDtypeStruct((N_RET, DEC_BATCH, RET_HEADS, RET_DK, RET_DV), F32)],
        input_output_aliases=aliases,
        compiler_params=_params("parallel", "parallel"), name="ret_sample",
    )(*operands)


def _rms_norm_gate(o, gain, gate):
    return o * lax.rsqrt(jnp.mean(o * o, axis=-1, keepdims=True) + LN_EPS) * gain * gate


def _diag_tiles(g2_blk, q_blk, g2_row, k_row):
    n = g2_blk.shape[0]
    return [jnp.exp2(g2_blk - g2_row(j)) * q_blk * k_row(j) for j in range(n)]


def _select_diag(sums, n, lane_base):
    lanes = sums.shape[1]
    rel = lax.broadcasted_iota(jnp.int32, (n, lanes), 1) - lane_base
    row = lax.broadcasted_iota(jnp.int32, (n, lanes), 0)
    acc = jnp.zeros((n, lanes), F32)
    for j in range(n):
        acc = jnp.where(rel == j, sums[j * n:(j + 1) * n, :], acc)
    return jnp.where(row >= rel, acc, 0.0)


def _gla_prompt_body(q_ref, k_ref, lf_ref, v_ref, g_ref, gain_ref, ltri_ref, o_ref, st_ref, s_scr, g_scr):
    ci = pl.program_id(2)
    c = GLA_C
    n = GLA_SUB

    @pl.when(ci == 0)
    def _():
        s_scr[...] = jnp.zeros_like(s_scr)

    width = q_ref.shape[1]
    heads = [slice(hh * HG_DK, (hh + 1) * HG_DK) for hh in range(GLA_HB)]
    ones_b = jnp.ones((HG_DK, c), BF16)
    row = lax.broadcasted_iota(jnp.int32, (c, width), 0)
    ri = lax.broadcasted_iota(jnp.int32, (c, c), 0)
    cj = lax.broadcasted_iota(jnp.int32, (c, c), 1)

    g2 = LOG2_E * jnp.dot(ltri_ref[...], lf_ref[...], precision=lax.Precision.HIGHEST,
                          preferred_element_type=F32)
    g_scr[...] = g2
    q = q_ref[...]
    k = k_ref[...]

    a = [jnp.zeros((c, c), F32) for _ in heads]
    s = c // 2
    while s >= n:
        ref_rows = [jnp.broadcast_to(g_scr[m + s - 1:m + s, :], (2 * s, width)) for m in range(0, c, 2 * s)]
        ref = ref_rows[0] if len(ref_rows) == 1 else jnp.concatenate(ref_rows, axis=0)
        e = jnp.exp2(-jnp.abs(g2 - ref))
        second = (row & s) != 0
        qt = jnp.where(second, q * e, 0.0).astype(BF16)
        kt = jnp.where(second, 0.0, k * e).astype(BF16)
        for hh, sl in enumerate(heads):
            p = lax.dot_general(qt[:, sl], kt[:, sl], (((1,), (1,)), ((), ())), preferred_element_type=F32)
            if 2 * s < c:
                sh = int(math.log2(2 * s))
                p = jnp.where((ri >> sh) == (cj >> sh), p, 0.0)
            a[hh] = a[hh] + p
        s //= 2

    tiles = []
    for sl in heads:
        for m in range(0, c, n):
            tiles += _diag_tiles(g2[m:m + n, sl], q[m:m + n, sl],
                                 lambda j, m=m, sl=sl: g_scr[m + j:m + j + 1, sl],
                                 lambda j, m=m, sl=sl: k_ref[m + j:m + j + 1, sl])
    sums = jnp.dot(jnp.concatenate(tiles, axis=0).astype(BF16), ones_b, preferred_element_type=F32)

    qe = (q * jnp.exp2(g2)).astype(BF16)
    g2_last = g_scr[c - 1:c, :]
    kd = (k * jnp.exp2(g2_last - g2)).astype(BF16)
    decay = jnp.exp2(g2_last)
    outs = []
    for hh, sl in enumerate(heads):
        base = hh * c * n
        diag = [_select_diag(sums[base + m * n:base + (m + n) * n, :], n, m) for m in range(0, c, n)]
        a_h = (a[hh] + jnp.concatenate(diag, axis=0)).astype(BF16)
        v = v_ref[:, sl]
        s_t = s_scr[hh]
        o = jnp.dot(a_h, v, preferred_element_type=F32)
        o = o + lax.dot_general(qe[:, sl], s_t.astype(BF16), (((1,), (1,)), ((), ())),
                                preferred_element_type=F32)
        s_scr[hh] = decay[:, sl] * s_t + lax.dot_general(
            v, kd[:, sl], (((0,), (0,)), ((), ())), preferred_element_type=F32)
        outs.append(_rms_norm_gate(o, gain_ref[...], g_ref[:, sl]))

    o_ref[...] = jnp.concatenate(outs, axis=1).astype(BF16)

    @pl.when(ci == pl.num_programs(2) - 1)
    def _():
        for hh in range(GLA_HB):
            st_ref[0, hh] = s_scr[hh].T


def _gla_prompt(q, k, lf, v, g, gain):
    c = GLA_C
    nc = SEQ // c
    w = GLA_HB * HG_DK
    ltri = (jnp.arange(c)[:, None] >= jnp.arange(c)[None, :]).astype(F32)
    row = lambda b, hb, ci: (b * nc + ci, hb)
    return pl.pallas_call(
        _gla_prompt_body,
        grid=(BATCH, HG_HEADS // GLA_HB, nc),
        in_specs=[pl.BlockSpec((c, w), row), pl.BlockSpec((c, w), row), pl.BlockSpec((c, w), row),
                  pl.BlockSpec((c, w), row), pl.BlockSpec((c, w), row),
                  pl.BlockSpec((1, HG_DV), lambda b, hb, ci: (0, 0)),
                  pl.BlockSpec((c, c), lambda b, hb, ci: (0, 0))],
        out_specs=[pl.BlockSpec((c, w), row),
                   pl.BlockSpec((1, GLA_HB, HG_DK, HG_DV), lambda b, hb, ci: (b, hb, 0, 0))],
        out_shape=[jax.ShapeDtypeStruct((M_ROWS, D_MODEL), BF16),
                   jax.ShapeDtypeStruct((BATCH, HG_HEADS, HG_DK, HG_DV), F32)],
        scratch_shapes=[pltpu.VMEM((GLA_HB, HG_DV, HG_DK), F32), pltpu.VMEM((c, w), F32)],
        compiler_params=_params("parallel", "parallel", "arbitrary"), name="gla_prompt",
    )(q, k, lf, v, g, gain.reshape(1, HG_DV), ltri)


def _gla_sample_body(q_ref, k_ref, lf_ref, v_ref, g_ref, gain_ref, s0_ref, o_prev_ref, st_prev_ref,
                     o_ref, s_ref, g_scr):
    del o_prev_ref, st_prev_ref
    n = DEC_SEQ
    rows_per = SAMPLE_BB * n
    width = q_ref.shape[1]
    rix = lax.broadcasted_iota(jnp.int32, (rows_per, width), 0) & (n - 1)
    g2_all = lf_ref[...]
    sh = 1
    while sh < n:
        g2_all = g2_all + jnp.where(rix >= sh, pltpu.roll(g2_all, sh, axis=0), 0.0)
        sh *= 2
    g2_all = LOG2_E * g2_all
    g_scr[...] = g2_all
    q_all = q_ref[...]
    g2_last = jnp.concatenate([jnp.broadcast_to(g_scr[(bi + 1) * n - 1:(bi + 1) * n, :], (n, width))
                               for bi in range(SAMPLE_BB)], axis=0)
    qe_all = q_all * jnp.exp2(g2_all)
    kd_all = k_ref[...] * jnp.exp2(g2_last - g2_all)
    decay_all = jnp.exp2(g2_last)
    v_all = v_ref[...].astype(F32)
    ones_b = jnp.ones((HG_DK, HG_DK), BF16)
    zpad = jnp.zeros((HG_DK - n, HG_DV), F32)
    units = [(bi, h) for bi in range(SAMPLE_BB) for h in range(HG_HEADS)]
    span = lambda bi, h: (slice(bi * n, (bi + 1) * n), slice(h * HG_DK, (h + 1) * HG_DK))

    tiles = []
    for bi, h in units:
        rows, sl = span(bi, h)
        tiles += _diag_tiles(g2_all[rows, sl], q_all[rows, sl],
                             lambda j, bi=bi, sl=sl: g_scr[bi * n + j:bi * n + j + 1, sl],
                             lambda j, bi=bi, sl=sl: k_ref[bi * n + j:bi * n + j + 1, sl])
    sums = jnp.dot(jnp.concatenate(tiles, axis=0).astype(BF16), ones_b, preferred_element_type=F32)

    v_pads, o_units = [], []
    for u, (bi, h) in enumerate(units):
        rows, sl = span(bi, h)
        a = _select_diag(sums[u * n * n:(u + 1) * n * n, :], n, 0)
        v_pad = jnp.concatenate([v_all[rows, sl], zpad], axis=0).astype(BF16)
        v_pads.append(v_pad)
        o = jnp.dot(a.astype(BF16), v_pad, preferred_element_type=F32)
        o_units.append(o + jnp.dot(qe_all[rows, sl].astype(BF16), s0_ref[0, bi, h].astype(BF16),
                                   preferred_element_type=F32))
    for u, (bi, h) in enumerate(units):
        rows, sl = span(bi, h)
        kd_pad = jnp.concatenate([kd_all[rows, sl], zpad], axis=0).astype(BF16)
        decay_col = jnp.broadcast_to(decay_all[bi * n:bi * n + 1, sl], (HG_DV, HG_DK)).T
        s_ref[0, bi, h] = decay_col * s0_ref[0, bi, h] + lax.dot_general(
            kd_pad, v_pads[u], (((0,), (0,)), ((), ())), preferred_element_type=F32)
    outs = []
    for bi in range(SAMPLE_BB):
        per_head = [_rms_norm_gate(o_units[bi * HG_HEADS + h], gain_ref[...], g_ref[span(bi, h)])
                    for h in range(HG_HEADS)]
        outs.append(jnp.concatenate(per_head, axis=1))
    o_ref[...] = jnp.concatenate(outs, axis=0).astype(BF16)


def _gla_sample(q, k, lf, v, g, gain, state_all, layer_j, o_prev, st_prev):
    rows_per = SAMPLE_BB * DEC_SEQ
    row0 = MP_ROWS // rows_per
    row = lambda bb: (row0 + bb, 0)
    st_block = (1, SAMPLE_BB, HG_HEADS, HG_DK, HG_DV)
    st_map = lambda bb: (layer_j, bb, 0, 0, 0)
    in_specs = [pl.BlockSpec((rows_per, D_MODEL), row)] * 5 + [
        pl.BlockSpec((1, HG_DV), lambda bb: (0, 0)),
        pl.BlockSpec(st_block, st_map),
        pl.BlockSpec(memory_space=pl.ANY)]
    operands = [q, k, lf, v, g, gain.reshape(1, HG_DV), state_all, o_prev]
    aliases = {7: 0}
    if st_prev is not None:
        in_specs.append(pl.BlockSpec(memory_space=pl.ANY))
        operands.append(st_prev)
        aliases[8] = 1
        body = _gla_sample_body
    else:
        body = lambda *refs: _gla_sample_body(*refs[:8], None, *refs[8:])
    return pl.pallas_call(
        body,
        grid=(DEC_BATCH // SAMPLE_BB,),
        in_specs=in_specs,
        out_specs=[pl.BlockSpec((rows_per, D_MODEL), row), pl.BlockSpec(st_block, st_map)],
        out_shape=[jax.ShapeDtypeStruct((M_ROWS, D_MODEL), BF16),
                   jax.ShapeDtypeStruct((N_HGRN, DEC_BATCH, HG_HEADS, HG_DK, HG_DV), F32)],
        input_output_aliases=aliases,
        scratch_shapes=[pltpu.VMEM((rows_per, D_MODEL), F32)],
        compiler_params=_params("parallel"), name="gla_sample",
    )(*operands)


def _rope_tables():
    inv = 1.0 / (ROPE_BASE ** jnp.linspace(0.0, 1.0, RET_DK // 2, dtype=F32))
    pos = jnp.concatenate([jnp.tile(0 + jnp.arange(SEQ, dtype=F32), BATCH),
                           jnp.tile(PAST_LEN + jnp.arange(DEC_SEQ, dtype=F32), DEC_BATCH)])
    ang = pos[:, None] * inv[None, :]
    cos = jnp.repeat(jnp.cos(ang), 2, axis=-1)
    sin = jnp.repeat(jnp.sin(ang), 2, axis=-1)
    sign = jnp.tile(jnp.array([-1.0, 1.0], F32), RET_DK // 2)
    return cos, sin * sign[None, :]


def _retention_layer(xb, w_in, gn_gain, state_all, layer_j, st_prev, rope):
    cos, sin = rope
    rope_specs = [pl.BlockSpec((TM, RET_DK), lambda j, i: (i, 0))] * 2
    (q,) = _dense(xb, w_in, layer_j, [0], RET_KD, functools.partial(_ep_rotary, scale=1.0), [F32],
                  extras=(cos, sin), extra_specs=rope_specs, tn=TN_WIDE, name="ret_q")
    (k,) = _dense(xb, w_in, layer_j, [RET_KD], RET_KD,
                  functools.partial(_ep_rotary, scale=RET_DK ** -0.5), [F32],
                  extras=(cos, sin), extra_specs=rope_specs, tn=TN_WIDE, name="ret_k")
    (v,) = _dense(xb, w_in, layer_j, [2 * RET_KD], RET_VD, _ep_plain, [BF16], tn=TN_WIDE, name="ret_v")
    (g,) = _dense(xb, w_in, layer_j, [2 * RET_KD + RET_VD], RET_VD, _ep_silu, [F32], tn=TN_WIDE,
                  name="ret_g")
    o, st_p = _ret_prompt(q, k, v, g, gn_gain)
    o, st_s = _ret_sample(q, k, v, g, gn_gain, state_all, layer_j, o, st_prev)
    return o, st_p, st_s


def _hgrn_layer(xb, w_in, lb_logits, norm_gain, state_all, layer_j, st_prev):
    d = D_MODEL
    logit_spec = [pl.BlockSpec((N_HGRN, TN), lambda j, i: (0, j))]
    (q,) = _dense(xb, w_in, layer_j, [0], d, functools.partial(_ep_silu_scaled, scale=HG_DK ** -0.5),
                  [F32], tn=TN_WIDE, name="hg_q")
    lf, k = _dense(xb, w_in, layer_j, [d], d, functools.partial(_ep_forget, layer=layer_j), [F32, F32],
                   extras=(lb_logits,), extra_specs=logit_spec, name="hg_f")
    (v,) = _dense(xb, w_in, layer_j, [2 * d], d, _ep_plain, [BF16], tn=TN_WIDE, name="hg_v")
    (g,) = _dense(xb, w_in, layer_j, [3 * d], d, _ep_silu, [F32], tn=TN_WIDE, name="hg_g")
    o, st_p = _gla_prompt(q, k, lf, v, g, norm_gain)
    o, st_s = _gla_sample(q, k, lf, v, g, norm_gain, state_all, layer_j, o, st_prev)
    return o, st_p, st_s


def kernel(x_prompt, x_sample, state_ret, state_hgrn, ret_w_in, ret_gn_gain, ret_w_out,
           hgrn_w_in, hgrn_lb_logits, hgrn_norm_gain, hgrn_w_out,
           ln_mix_g, ln_mix_b, ffn_w_in, ffn_w_out, ln_ffn_g, ln_ffn_b):
    assert x_prompt.shape == (BATCH, SEQ, D_MODEL) and x_sample.shape == (DEC_BATCH, DEC_SEQ, D_MODEL)
    assert state_ret.shape == (N_RET, DEC_BATCH, RET_HEADS, RET_DK, RET_DV)
    assert state_hgrn.shape == (N_HGRN, DEC_BATCH, HG_HEADS, HG_DK, HG_DV)
    assert ffn_w_in.shape == (DEPTH, D_MODEL, 2 * D_FF)

    x = jnp.concatenate([x_prompt.reshape(MP_ROWS, D_MODEL), x_sample.reshape(MS_ROWS, D_MODEL)], axis=0)
    xb = x.astype(BF16)
    rope = _rope_tables()
    ret_w_out_b = ret_w_out.astype(BF16)
    hgrn_w_out_b = hgrn_w_out.astype(BF16)
    ffn_w_out_b = ffn_w_out.astype(BF16)

    ret_p, hg_p = [], []
    ret_s = None
    hg_s = None
    for layer in range(DEPTH):
        j = layer // 2
        if layer % 2 == 0:
            o, st_p, ret_s = _retention_layer(xb, ret_w_in, ret_gn_gain[j], state_ret, j, ret_s, rope)
            ret_p.append(st_p)
            w_out = ret_w_out_b
        else:
            o, st_p, hg_s = _hgrn_layer(xb, hgrn_w_in, hgrn_lb_logits, hgrn_norm_gain[j],
                                        state_hgrn, j, hg_s)
            hg_p.append(st_p)
            w_out = hgrn_w_out_b
        x, xb = _dense_ln(o, w_out, j, x, ln_mix_g[layer], ln_mix_b[layer], name="mix_out")
        (h,) = _dense(xb, ffn_w_in, layer, [0, D_FF], D_FF, _ep_swiglu, [BF16], name="ffn_in")
        x, xb = _dense_ln(h, ffn_w_out_b, layer, x, ln_ffn_g[layer], ln_ffn_b[layer], name="ffn_out")

    return (x[:MP_ROWS].reshape(BATCH, SEQ, D_MODEL),
            x[MP_ROWS:].reshape(DEC_BATCH, DEC_SEQ, D_MODEL),
            jnp.stack(ret_p), ret_s, jnp.stack(hg_p), hg_s)
```

```python
import functools
import math

import jax
import jax.numpy as jnp
from jax import lax
from jax.experimental import pallas as pl
from jax.experimental.pallas import tpu as pltpu

F32 = jnp.float32
BF16 = jnp.bfloat16

D_MODEL = 2048
BATCH = 4
SEQ = 2048
DEPTH = 4
DEC_BATCH = 128
DEC_SEQ = 8
PAST_LEN = 16384
N_RET = 2
N_HGRN = 2
RET_HEADS = 8
RET_DK = 256
RET_DV = 512
RET_KD = RET_HEADS * RET_DK
RET_VD = RET_HEADS * RET_DV
HG_HEADS = 16
HG_DK = 128
HG_DV = 128
D_FF = 5632
ROPE_BASE = 10000.0
ALPHA = (2 * DEPTH) ** 0.25
LN_EPS = 1e-5
LB_FLOOR = 1e-30
LOG2_E = 1.4426950408889634

MP_ROWS = BATCH * SEQ
MS_ROWS = DEC_BATCH * DEC_SEQ
M_ROWS = MP_ROWS + MS_ROWS

VMEM_LIMIT_BYTES = 56 * 1024 * 1024

LANES = 128

TM = 1024
TN_WIDE = 1024
TN = 512
TM_LN = 1024
LN_ROWS = 256
EPI_COLS = 256

RET_C = 256
RET_HB = 4
GLA_C = 128
GLA_SUB = 8
GLA_HB = 8
SAMPLE_BB = 2
RET_S_HB = 4


def _params(*sem):
    return pltpu.CompilerParams(dimension_semantics=sem, vmem_limit_bytes=VMEM_LIMIT_BYTES)


def _sigmoid(x):
    return 1.0 / (1.0 + jnp.exp(-x))


def _silu(x):
    return x * _sigmoid(x)


def _dense_body(*refs, n_w, n_extra, n_out, epilogue):
    x_ref = refs[0]
    w_refs = refs[1:1 + n_w]
    extra_refs = refs[1 + n_w:1 + n_w + n_extra]
    out_refs = refs[1 + n_w + n_extra:1 + n_w + n_extra + n_out]
    wb_refs = refs[1 + n_w + n_extra + n_out:]

    if wb_refs:
        @pl.when(pl.program_id(1) == 0)
        def _():
            for w_ref, wb_ref in zip(w_refs, wb_refs):
                wb_ref[...] = w_ref[...].astype(BF16)
        rhs = [lambda cols, r=wb_ref: r[:, cols] for wb_ref in wb_refs]
    else:
        rhs = [lambda cols, r=w_ref: r[:, cols].astype(BF16) for w_ref in w_refs]

    x = x_ref[...]
    for c0 in range(0, out_refs[0].shape[1], EPI_COLS):
        cols = slice(c0, c0 + EPI_COLS)
        accs = [jnp.dot(x, get(cols), preferred_element_type=F32) for get in rhs]
        epilogue(accs, extra_refs, out_refs, cols)


def _dense(x, w, layer, col_offs, n_cols, epilogue, out_dtypes, extras=(), extra_specs=(), tn=TN,
           hoist_cast=True, name="dense"):
    m, k = x.shape
    assert m % TM == 0 and n_cols % tn == 0 and all(c % tn == 0 for c in col_offs)
    grid = (n_cols // tn, m // TM)
    in_specs = [pl.BlockSpec((TM, k), lambda j, i: (i, 0))]
    for off in col_offs:
        in_specs.append(pl.BlockSpec((None, k, tn),
                                     functools.partial(lambda j, i, o: (layer, 0, j + o), o=off // tn)))
    in_specs += list(extra_specs)
    out_specs = [pl.BlockSpec((TM, tn), lambda j, i: (i, j)) for _ in out_dtypes]
    out_shape = [jax.ShapeDtypeStruct((m, n_cols), dt) for dt in out_dtypes]
    body = functools.partial(_dense_body, n_w=len(col_offs), n_extra=len(extras),
                             n_out=len(out_dtypes), epilogue=epilogue)
    return pl.pallas_call(
        body, grid=grid, in_specs=in_specs, out_specs=out_specs, out_shape=out_shape,
        scratch_shapes=[pltpu.VMEM((k, tn), BF16) for _ in col_offs] if hoist_cast else [],
        compiler_params=_params("arbitrary", "arbitrary"), name=name,
    )(x, *([w] * len(col_offs)), *extras)


def _ep_plain(accs, extra_refs, out_refs, cols):
    out_refs[0][:, cols] = accs[0].astype(out_refs[0].dtype)


def _ep_silu(accs, extra_refs, out_refs, cols):
    out_refs[0][:, cols] = _silu(accs[0]).astype(out_refs[0].dtype)


def _ep_silu_scaled(accs, extra_refs, out_refs, cols, *, scale):
    out_refs[0][:, cols] = (_silu(accs[0]) * scale).astype(out_refs[0].dtype)


def _ep_swiglu(accs, extra_refs, out_refs, cols):
    out_refs[0][:, cols] = (_silu(accs[0]) * accs[1]).astype(out_refs[0].dtype)


def _ep_rotary(accs, extra_refs, out_refs, cols, *, scale):
    cos_ref, sin_ref = extra_refs
    t = accs[0]
    assert t.shape[1] == RET_DK
    lane = lax.broadcasted_iota(jnp.int32, t.shape, 1)
    swapped = jnp.where((lane & 1) == 0,
                        pltpu.roll(t, t.shape[1] - 1, axis=1),
                        pltpu.roll(t, 1, axis=1))
    r = t * cos_ref[...] + swapped * sin_ref[...]
    if scale != 1.0:
        r = r * scale
    out_refs[0][:, cols] = r.astype(out_refs[0].dtype)


def _ep_forget(accs, extra_refs, out_refs, cols, *, layer):
    logits_ref = extra_refs[0]
    rows = [logits_ref[r:r + 1, cols] for r in range(N_HGRN)]
    mx = functools.reduce(jnp.maximum, rows)
    es = [jnp.exp(r - mx) for r in rows]
    den = functools.reduce(lambda a, b: a + b, es)
    ps = [e / den for e in es]
    lb = functools.reduce(lambda a, b: a + b, ps[:layer + 1]) - ps[0]
    fpre = accs[0]
    t = jnp.exp(-jnp.abs(fpre))
    r = 1.0 / (1.0 + t)
    tr = t * r
    pos = fpre >= 0.0
    one_m_lb = 1.0 - lb
    out_refs[0][:, cols] = jnp.log(jnp.maximum(lb, LB_FLOOR) + one_m_lb * jnp.where(pos, r, tr))
    out_refs[1][:, cols] = one_m_lb * jnp.where(pos, tr, r)


def _dense_ln_body(h_ref, w_ref, x_hbm, g_ref, b_ref, of_ref, ob_ref, x_buf, x_sem, *, n_k):
    i = pl.program_id(0)
    kk = pl.program_id(1)
    tm = of_ref.shape[0]
    x_copy = pltpu.make_async_copy(x_hbm.at[pl.ds(pl.multiple_of(i * tm, tm), tm), :], x_buf, x_sem)

    @pl.when(kk == 0)
    def _():
        x_copy.start()
        of_ref[...] = jnp.dot(h_ref[...], w_ref[...], preferred_element_type=F32)

    @pl.when(kk > 0)
    def _():
        of_ref[...] += jnp.dot(h_ref[...], w_ref[...], preferred_element_type=F32)

    @pl.when(kk == n_k - 1)
    def _():
        x_copy.wait()
        for r0 in range(0, tm, LN_ROWS):
            rows = slice(r0, r0 + LN_ROWS)
            y = of_ref[rows, :] + ALPHA * x_buf[rows, :]
            mu = jnp.mean(y, axis=-1, keepdims=True)
            d = y - mu
            var = jnp.mean(d * d, axis=-1, keepdims=True)
            r = d * lax.rsqrt(var + LN_EPS) * g_ref[...] + b_ref[...]
            of_ref[rows, :] = r
            ob_ref[rows, :] = r.astype(BF16)


def _ln_contraction_tile(k):
    cap = (6 * 1024 * 1024) // (2 * D_MODEL)
    return max(t for t in range(LANES, cap + 1, LANES) if k % t == 0)


def _dense_ln(h, w_bf16, layer, x_res, gain, bias, name):
    m, k = h.shape
    d = w_bf16.shape[2]
    tk = _ln_contraction_tile(k)
    assert m % TM_LN == 0 and k % tk == 0
    n_k = k // tk
    return pl.pallas_call(
        functools.partial(_dense_ln_body, n_k=n_k),
        grid=(m // TM_LN, n_k),
        in_specs=[pl.BlockSpec((TM_LN, tk), lambda i, kk: (i, kk)),
                  pl.BlockSpec((None, tk, d), lambda i, kk: (layer, kk, 0)),
                  pl.BlockSpec(memory_space=pl.ANY),
                  pl.BlockSpec((1, d), lambda i, kk: (0, 0)),
                  pl.BlockSpec((1, d), lambda i, kk: (0, 0))],
        out_specs=[pl.BlockSpec((TM_LN, d), lambda i, kk: (i, 0)),
                   pl.BlockSpec((TM_LN, d), lambda i, kk: (i, 0))],
        out_shape=[jax.ShapeDtypeStruct((m, d), F32), jax.ShapeDtypeStruct((m, d), BF16)],
        scratch_shapes=[pltpu.VMEM((TM_LN, d), F32), pltpu.SemaphoreType.DMA(())],
        compiler_params=_params("arbitrary", "arbitrary"), name=name,
    )(h, w_bf16, x_res, gain.reshape(1, d), bias.reshape(1, d))


def _group_norm_gate(o, gain, gate):
    mu = jnp.mean(o, axis=-1, keepdims=True)
    d = o - mu
    var = jnp.mean(d * d, axis=-1, keepdims=True)
    return gate * (d * lax.rsqrt(var + LN_EPS) * gain)


def _ret_step(q, k, v, s, dmat, qdec, kdec, cdec):
    a = lax.dot_general(q.astype(BF16), k.astype(BF16), (((1,), (1,)), ((), ())),
                        preferred_element_type=F32) * dmat
    o = jnp.dot(a.astype(BF16), v, preferred_element_type=F32)
    o = o + jnp.dot((q * qdec).astype(BF16), s.astype(BF16), preferred_element_type=F32)
    s_new = cdec * s + lax.dot_general((k * kdec).astype(BF16), v, (((0,), (0,)), ((), ())),
                                       preferred_element_type=F32)
    return o, s_new


def _ret_prompt_body(q_ref, k_ref, v_ref, g_ref, gain_ref, dmat_ref, qdec_ref, kdec_ref, cdec_ref,
                     o_ref, s_ref):
    @pl.when(pl.program_id(2) == 0)
    def _():
        s_ref[...] = jnp.zeros_like(s_ref)

    heads = range(RET_HB)
    ks = [slice(h * RET_DK, (h + 1) * RET_DK) for h in heads]
    vs = [slice(h * RET_DV, (h + 1) * RET_DV) for h in heads]
    q = q_ref[...]
    k = k_ref[...]
    qb = q.astype(BF16)
    kb = k.astype(BF16)
    qd = (q * qdec_ref[...]).astype(BF16)
    kd = (k * kdec_ref[...]).astype(BF16)
    a = [(lax.dot_general(qb[:, ks[h]], kb[:, ks[h]], (((1,), (1,)), ((), ())),
                          preferred_element_type=F32) * dmat_ref[h]).astype(BF16) for h in heads]
    s_old = [s_ref[0, h] for h in heads]
    o = [jnp.dot(a[h], v_ref[:, vs[h]], preferred_element_type=F32)
         + jnp.dot(qd[:, ks[h]], s_old[h].astype(BF16), preferred_element_type=F32) for h in heads]
    for h in heads:
        s_ref[0, h] = cdec_ref[:, vs[h]] * s_old[h] + lax.dot_general(
            kd[:, ks[h]], v_ref[:, vs[h]], (((0,), (0,)), ((), ())), preferred_element_type=F32)
    o_ref[...] = jnp.concatenate(
        [_group_norm_gate(o[h], gain_ref[:, vs[h]], g_ref[:, vs[h]]) for h in heads], axis=1).astype(BF16)


def _ret_decay_tables(c):
    lg = jnp.log(1.0 - 2.0 ** (-5.0 - jnp.arange(RET_HEADS, dtype=F32)))
    idx = jnp.arange(c, dtype=F32)
    rel = idx[:, None] - idx[None, :]
    causal = rel >= 0
    dmat = jnp.where(causal[None], jnp.exp(jnp.where(causal, rel, 0.0)[None] * lg[:, None, None]), 0.0)
    qdec = jnp.exp((idx + 1)[None, :] * lg[:, None])
    kdec = jnp.exp((c - 1 - idx)[None, :] * lg[:, None])
    cdec = jnp.exp(c * lg)
    qdec = jnp.broadcast_to(qdec[:, :, None], (RET_HEADS, c, RET_DK))
    kdec = jnp.broadcast_to(kdec[:, :, None], (RET_HEADS, c, RET_DK))
    cdec = jnp.broadcast_to(cdec[:, None, None], (RET_HEADS, 1, RET_DV))
    return dmat, qdec, kdec, cdec


def _ret_prompt(q, k, v, g, gain):
    c = RET_C
    nc = SEQ // c
    dmat, qdec, kdec, cdec = _ret_decay_tables(c)
    qdec = jnp.transpose(qdec, (1, 0, 2)).reshape(c, RET_KD)
    kdec = jnp.transpose(kdec, (1, 0, 2)).reshape(c, RET_KD)
    cdec = jnp.transpose(cdec, (1, 0, 2)).reshape(1, RET_VD)
    kw = RET_HB * RET_DK
    vw = RET_HB * RET_DV
    row = lambda b, h, ci: (b * nc + ci, h)
    lane = lambda b, h, ci: (0, h)
    return pl.pallas_call(
        _ret_prompt_body,
        grid=(BATCH, RET_HEADS // RET_HB, nc),
        in_specs=[pl.BlockSpec((c, kw), row), pl.BlockSpec((c, kw), row),
                  pl.BlockSpec((c, vw), row), pl.BlockSpec((c, vw), row),
                  pl.BlockSpec((1, vw), lane),
                  pl.BlockSpec((RET_HB, c, c), lambda b, h, ci: (h, 0, 0)), pl.BlockSpec((c, kw), lane),
                  pl.BlockSpec((c, kw), lane), pl.BlockSpec((1, vw), lane)],
        out_specs=[pl.BlockSpec((c, vw), row),
                   pl.BlockSpec((1, RET_HB, RET_DK, RET_DV), lambda b, h, ci: (b, h, 0, 0))],
        out_shape=[jax.ShapeDtypeStruct((M_ROWS, RET_VD), BF16),
                   jax.ShapeDtypeStruct((BATCH, RET_HEADS, RET_DK, RET_DV), F32)],
        compiler_params=_params("parallel", "parallel", "arbitrary"), name="ret_prompt",
    )(q, k, v, g, gain.reshape(1, RET_VD), dmat, qdec, kdec, cdec)


def _ret_sample_body(q_ref, k_ref, v_ref, g_ref, gain_ref, dmat_ref, qdec_ref, kdec_ref, cdec_ref,
                     s0_ref, o_prev_ref, st_prev_ref, o_ref, s_ref):
    del o_prev_ref, st_prev_ref
    hb = pl.program_id(1)
    v_all = v_ref[...].astype(F32)
    outs = []
    for bi in range(SAMPLE_BB):
        rows = slice(bi * DEC_SEQ, (bi + 1) * DEC_SEQ)
        per_head = []
        for h in range(RET_S_HB):
            ks = slice(h * RET_DK, (h + 1) * RET_DK)
            vs = slice(h * RET_DV, (h + 1) * RET_DV)
            o, s_new = _ret_step(q_ref[rows, ks], k_ref[rows, ks], v_all[rows, vs].astype(BF16),
                                 s0_ref[0, bi, h], dmat_ref[h], qdec_ref[h], kdec_ref[h], cdec_ref[h])
            s_ref[0, bi, h] = s_new
            per_head.append(_group_norm_gate(o, gain_ref[:, vs], g_ref[rows, vs]))
        outs.append(jnp.concatenate(per_head, axis=1))
    del hb
    o_ref[...] = jnp.concatenate(outs, axis=0).astype(BF16)


def _ret_sample(q, k, v, g, gain, state_all, layer_j, o_prev, st_prev):
    c = DEC_SEQ
    dmat, qdec, kdec, cdec = _ret_decay_tables(c)
    rows_per = SAMPLE_BB * DEC_SEQ
    row0 = MP_ROWS // rows_per
    n_hb = RET_HEADS // RET_S_HB
    row = lambda bb, hb: (row0 + bb, hb)
    hsel = lambda bb, hb: (hb, 0, 0)
    st_block = (1, SAMPLE_BB, RET_S_HB, RET_DK, RET_DV)
    st_map = lambda bb, hb: (layer_j, bb, hb, 0, 0)
    in_specs = [pl.BlockSpec((rows_per, RET_S_HB * RET_DK), row),
                pl.BlockSpec((rows_per, RET_S_HB * RET_DK), row),
                pl.BlockSpec((rows_per, RET_S_HB * RET_DV), row),
                pl.BlockSpec((rows_per, RET_S_HB * RET_DV), row),
                pl.BlockSpec((1, RET_S_HB * RET_DV), lambda bb, hb: (0, hb)),
                pl.BlockSpec((RET_S_HB, c, c), hsel),
                pl.BlockSpec((RET_S_HB, c, RET_DK), hsel),
                pl.BlockSpec((RET_S_HB, c, RET_DK), hsel),
                pl.BlockSpec((RET_S_HB, 1, RET_DV), hsel),
                pl.BlockSpec(st_block, st_map),
                pl.BlockSpec(memory_space=pl.ANY)]
    operands = [q, k, v, g, gain.reshape(1, RET_VD), dmat, qdec, kdec, cdec, state_all, o_prev]
    aliases = {10: 0}
    if st_prev is not None:
        in_specs.append(pl.BlockSpec(memory_space=pl.ANY))
        operands.append(st_prev)
        aliases[11] = 1
        body = _ret_sample_body
    else:
        body = lambda *refs: _ret_sample_body(*refs[:11], None, *refs[11:])
    return pl.pallas_call(
        body,
        grid=(DEC_BATCH // SAMPLE_BB, n_hb),
        in_specs=in_specs,
        out_specs=[pl.BlockSpec((rows_per, RET_S_HB * RET_DV), row), pl.BlockSpec(st_block, st_map)],
        out_shape=[jax.ShapeDtypeStruct((M_ROWS, RET_VD), BF16),
                   jax.ShapeDtypeStruct((N_RET, DEC_BATCH, RET_HEADS, RET_DK, RET_DV), F32)],
        input_output_aliases=aliases,
        compiler_params=_params("parallel", "parallel"), name="ret_sample",
    )(*operands)


def _rms_norm_gate(o, gain, gate):
    return o * lax.rsqrt(jnp.mean(o * o, axis=-1, keepdims=True) + LN_EPS) * gain * gate


def _diag_tiles(g2_blk, q_blk, g2_row, k_row):
    n = g2_blk.shape[0]
    return [jnp.exp2(g2_blk - g2_row(j)) * q_blk * k_row(j) for j in range(n)]


def _select_diag(sums, n, lane_base):
    lanes = sums.shape[1]
    rel = lax.broadcasted_iota(jnp.int32, (n, lanes), 1) - lane_base
    row = lax.broadcasted_iota(jnp.int32, (n, lanes), 0)
    acc = jnp.zeros((n, lanes), F32)
    for j in range(n):
        acc = jnp.where(rel == j, sums[j * n:(j + 1) * n, :], acc)
    return jnp.where(row >= rel, acc, 0.0)


def _gla_prompt_body(q_ref, k_ref, lf_ref, v_ref, g_ref, gain_ref, ltri_ref, o_ref, st_ref, s_scr, g_scr):
    ci = pl.program_id(2)
    c = GLA_C
    n = GLA_SUB

    @pl.when(ci == 0)
    def _():
        s_scr[...] = jnp.zeros_like(s_scr)

    width = q_ref.shape[1]
    heads = [slice(hh * HG_DK, (hh + 1) * HG_DK) for hh in range(GLA_HB)]
    ones_b = jnp.ones((HG_DK, c), BF16)
    row = lax.broadcasted_iota(jnp.int32, (c, width), 0)
    ri = lax.broadcasted_iota(jnp.int32, (c, c), 0)
    cj = lax.broadcasted_iota(jnp.int32, (c, c), 1)

    g2 = LOG2_E * jnp.dot(ltri_ref[...], lf_ref[...], precision=lax.Precision.HIGHEST,
                          preferred_element_type=F32)
    g_scr[...] = g2
    q = q_ref[...]
    k = k_ref[...]

    a = [jnp.zeros((c, c), F32) for _ in heads]
    s = c // 2
    while s >= n:
        ref_rows = [jnp.broadcast_to(g_scr[m + s - 1:m + s, :], (2 * s, width)) for m in range(0, c, 2 * s)]
        ref = ref_rows[0] if len(ref_rows) == 1 else jnp.concatenate(ref_rows, axis=0)
        e = jnp.exp2(-jnp.abs(g2 - ref))
        second = (row & s) != 0
        qt = jnp.where(second, q * e, 0.0).astype(BF16)
        kt = jnp.where(second, 0.0, k * e).astype(BF16)
        for hh, sl in enumerate(heads):
            p = lax.dot_general(qt[:, sl], kt[:, sl], (((1,), (1,)), ((), ())), preferred_element_type=F32)
            if 2 * s < c:
                sh = int(math.log2(2 * s))
                p = jnp.where((ri >> sh) == (cj >> sh), p, 0.0)
            a[hh] = a[hh] + p
        s //= 2

    tiles = []
    for sl in heads:
        for m in range(0, c, n):
            tiles += _diag_tiles(g2[m:m + n, sl], q[m:m + n, sl],
                                 lambda j, m=m, sl=sl: g_scr[m + j:m + j + 1, sl],
                                 lambda j, m=m, sl=sl: k_ref[m + j:m + j + 1, sl])
    sums = jnp.dot(jnp.concatenate(tiles, axis=0).astype(BF16), ones_b, preferred_element_type=F32)

    qe = (q * jnp.exp2(g2)).astype(BF16)
    g2_last = g_scr[c - 1:c, :]
    kd = (k * jnp.exp2(g2_last - g2)).astype(BF16)
    decay = jnp.exp2(g2_last)
    outs = []
    for hh, sl in enumerate(heads):
        base = hh * c * n
        diag = [_select_diag(sums[base + m * n:base + (m + n) * n, :], n, m) for m in range(0, c, n)]
        a_h = (a[hh] + jnp.concatenate(diag, axis=0)).astype(BF16)
        v = v_ref[:, sl]
        s_t = s_scr[hh]
        o = jnp.dot(a_h, v, preferred_element_type=F32)
        o = o + lax.dot_general(qe[:, sl], s_t.astype(BF16), (((1,), (1,)), ((), ())),
                                preferred_element_type=F32)
        s_scr[hh] = decay[:, sl] * s_t + lax.dot_general(
            v, kd[:, sl], (((0,), (0,)), ((), ())), preferred_element_type=F32)
        outs.append(_rms_norm_gate(o, gain_ref[...], g_ref[:, sl]))

    o_ref[...] = jnp.concatenate(outs, axis=1).astype(BF16)

    @pl.when(ci == pl.num_programs(2) - 1)
    def _():
        for hh in range(GLA_HB):
            st_ref[0, hh] = s_scr[hh].T


def _gla_prompt(q, k, lf, v, g, gain):
    c = GLA_C
    nc = SEQ // c
    w = GLA_HB * HG_DK
    ltri = (jnp.arange(c)[:, None] >= jnp.arange(c)[None, :]).astype(F32)
    row = lambda b, hb, ci: (b * nc + ci, hb)
    return pl.pallas_call(
        _gla_prompt_body,
        grid=(BATCH, HG_HEADS // GLA_HB, nc),
        in_specs=[pl.BlockSpec((c, w), row), pl.BlockSpec((c, w), row), pl.BlockSpec((c, w), row),
                  pl.BlockSpec((c, w), row), pl.BlockSpec((c, w), row),
                  pl.BlockSpec((1, HG_DV), lambda b, hb, ci: (0, 0)),
                  pl.BlockSpec((c, c), lambda b, hb, ci: (0, 0))],
        out_specs=[pl.BlockSpec((c, w), row),
                   pl.BlockSpec((1, GLA_HB, HG_DK, HG_DV), lambda b, hb, ci: (b, hb, 0, 0))],
        out_shape=[jax.ShapeDtypeStruct((M_ROWS, D_MODEL), BF16),
                   jax.ShapeDtypeStruct((BATCH, HG_HEADS, HG_DK, HG_DV), F32)],
        scratch_shapes=[pltpu.VMEM((GLA_HB, HG_DV, HG_DK), F32), pltpu.VMEM((c, w), F32)],
        compiler_params=_params("parallel", "parallel", "arbitrary"), name="gla_prompt",
    )(q, k, lf, v, g, gain.reshape(1, HG_DV), ltri)


def _gla_sample_body(q_ref, k_ref, lf_ref, v_ref, g_ref, gain_ref, s0_ref, o_prev_ref, st_prev_ref,
                     o_ref, s_ref, g_scr):
    del o_prev_ref, st_prev_ref
    n = DEC_SEQ
    rows_per = SAMPLE_BB * n
    width = q_ref.shape[1]
    rix = lax.broadcasted_iota(jnp.int32, (rows_per, width), 0) & (n - 1)
    g2_all = lf_ref[...]
    sh = 1
    while sh < n:
        g2_all = g2_all + jnp.where(rix >= sh, pltpu.roll(g2_all, sh, axis=0), 0.0)
        sh *= 2
    g2_all = LOG2_E * g2_all
    g_scr[...] = g2_all
    q_all = q_ref[...]
    g2_last = jnp.concatenate([jnp.broadcast_to(g_scr[(bi + 1) * n - 1:(bi + 1) * n, :], (n, width))
                               for bi in range(SAMPLE_BB)], axis=0)
    qe_all = q_all * jnp.exp2(g2_all)
    kd_all = k_ref[...] * jnp.exp2(g2_last - g2_all)
    decay_all = jnp.exp2(g2_last)
    v_all = v_ref[...].astype(F32)
    ones_b = jnp.ones((HG_DK, HG_DK), BF16)
    zpad = jnp.zeros((HG_DK - n, HG_DV), F32)
    units = [(bi, h) for bi in range(SAMPLE_BB) for h in range(HG_HEADS)]
    span = lambda bi, h: (slice(bi * n, (bi + 1) * n), slice(h * HG_DK, (h + 1) * HG_DK))

    tiles = []
    for bi, h in units:
        rows, sl = span(bi, h)
        tiles += _diag_tiles(g2_all[rows, sl], q_all[rows, sl],
                             lambda j, bi=bi, sl=sl: g_scr[bi * n + j:bi * n + j + 1, sl],
                             lambda j, bi=bi, sl=sl: k_ref[bi * n + j:bi * n + j + 1, sl])
    sums = jnp.dot(jnp.concatenate(tiles, axis=0).astype(BF16), ones_b, preferred_element_type=F32)

    v_pads, o_units = [], []
    for u, (bi, h) in enumerate(units):
        rows, sl = span(bi, h)
        a = _select_diag(sums[u * n * n:(u + 1) * n * n, :], n, 0)
        v_pad = jnp.concatenate([v_all[rows, sl], zpad], axis=0).astype(BF16)
        v_pads.append(v_pad)
        o = jnp.dot(a.astype(BF16), v_pad, preferred_element_type=F32)
        o_units.append(o + jnp.dot(qe_all[rows, sl].astype(BF16), s0_ref[0, bi, h].astype(BF16),
                                   preferred_element_type=F32))
    for u, (bi, h) in enumerate(units):
        rows, sl = span(bi, h)
        kd_pad = jnp.concatenate([kd_all[rows, sl], zpad], axis=0).astype(BF16)
        decay_col = jnp.broadcast_to(decay_all[bi * n:bi * n + 1, sl], (HG_DV, HG_DK)).T
        s_ref[0, bi, h] = decay_col * s0_ref[0, bi, h] + lax.dot_general(
            kd_pad, v_pads[u], (((0,), (0,)), ((), ())), preferred_element_type=F32)
    outs = []
    for bi in range(SAMPLE_BB):
        per_head = [_rms_norm_gate(o_units[bi * HG_HEADS + h], gain_ref[...], g_ref[span(bi, h)])
                    for h in range(HG_HEADS)]
        outs.append(jnp.concatenate(per_head, axis=1))
    o_ref[...] = jnp.concatenate(outs, axis=0).astype(BF16)


def _gla_sample(q, k, lf, v, g, gain, state_all, layer_j, o_prev, st_prev):
    rows_per = SAMPLE_BB * DEC_SEQ
    row0 = MP_ROWS // rows_per
    row = lambda bb: (row0 + bb, 0)
    st_block = (1, SAMPLE_BB, HG_HEADS, HG_DK, HG_DV)
    st_map = lambda bb: (layer_j, bb, 0, 0, 0)
    in_specs = [pl.BlockSpec((rows_per, D_MODEL), row)] * 5 + [
        pl.BlockSpec((1, HG_DV), lambda bb: (0, 0)),
        pl.BlockSpec(st_block, st_map),
        pl.BlockSpec(memory_space=pl.ANY)]
    operands = [q, k, lf, v, g, gain.reshape(1, HG_DV), state_all, o_prev]
    aliases = {7: 0}
    if st_prev is not None:
        in_specs.append(pl.BlockSpec(memory_space=pl.ANY))
        operands.append(st_prev)
        aliases[8] = 1
        body = _gla_sample_body
    else:
        body = lambda *refs: _gla_sample_body(*refs[:8], None, *refs[8:])
    return pl.pallas_call(
        body,
        grid=(DEC_BATCH // SAMPLE_BB,),
        in_specs=in_specs,
        out_specs=[pl.BlockSpec((rows_per, D_MODEL), row), pl.BlockSpec(st_block, st_map)],
        out_shape=[jax.ShapeDtypeStruct((M_ROWS, D_MODEL), BF16),
                   jax.ShapeDtypeStruct((N_HGRN, DEC_BATCH, HG_HEADS, HG_DK, HG_DV), F32)],
        input_output_aliases=aliases,
        scratch_shapes=[pltpu.VMEM((rows_per, D_MODEL), F32)],
        compiler_params=_params("parallel"), name="gla_sample",
    )(*operands)


def _rope_tables():
    inv = 1.0 / (ROPE_BASE ** jnp.linspace(0.0, 1.0, RET_DK // 2, dtype=F32))
    pos = jnp.concatenate([jnp.tile(0 + jnp.arange(SEQ, dtype=F32), BATCH),
                           jnp.tile(PAST_LEN + jnp.arange(DEC_SEQ, dtype=F32), DEC_BATCH)])
    ang = pos[:, None] * inv[None, :]
    cos = jnp.repeat(jnp.cos(ang), 2, axis=-1)
    sin = jnp.repeat(jnp.sin(ang), 2, axis=-1)
    sign = jnp.tile(jnp.array([-1.0, 1.0], F32), RET_DK // 2)
    return cos, sin * sign[None, :]


def _retention_layer(xb, w_in, gn_gain, state_all, layer_j, st_prev, rope):
    cos, sin = rope
    rope_specs = [pl.BlockSpec((TM, RET_DK), lambda j, i: (i, 0))] * 2
    (q,) = _dense(xb, w_in, layer_j, [0], RET_KD, functools.partial(_ep_rotary, scale=1.0), [F32],
                  extras=(cos, sin), extra_specs=rope_specs, tn=TN_WIDE, name="ret_q")
    (k,) = _dense(xb, w_in, layer_j, [RET_KD], RET_KD,
                  functools.partial(_ep_rotary, scale=RET_DK ** -0.5), [F32],
                  extras=(cos, sin), extra_specs=rope_specs, tn=TN_WIDE, name="ret_k")
    (v,) = _dense(xb, w_in, layer_j, [2 * RET_KD], RET_VD, _ep_plain, [BF16], tn=TN_WIDE, name="ret_v")
    (g,) = _dense(xb, w_in, layer_j, [2 * RET_KD + RET_VD], RET_VD, _ep_silu, [F32], tn=TN_WIDE,
                  name="ret_g")
    o, st_p = _ret_prompt(q, k, v, g, gn_gain)
    o, st_s = _ret_sample(q, k, v, g, gn_gain, state_all, layer_j, o, st_prev)
    return o, st_p, st_s


def _hgrn_layer(xb, w_in, lb_logits, norm_gain, state_all, layer_j, st_prev):
    d = D_MODEL
    logit_spec = [pl.BlockSpec((N_HGRN, TN_WIDE), lambda j, i: (0, j))]
    (q,) = _dense(xb, w_in, layer_j, [0], d, functools.partial(_ep_silu_scaled, scale=HG_DK ** -0.5),
                  [F32], tn=TN_WIDE, name="hg_q")
    lf, k = _dense(xb, w_in, layer_j, [d], d, functools.partial(_ep_forget, layer=layer_j), [F32, F32],
                   extras=(lb_logits,), extra_specs=logit_spec, tn=TN_WIDE, name="hg_f")
    (v,) = _dense(xb, w_in, layer_j, [2 * d], d, _ep_plain, [BF16], tn=TN_WIDE, name="hg_v")
    (g,) = _dense(xb, w_in, layer_j, [3 * d], d, _ep_silu, [F32], tn=TN_WIDE, name="hg_g")
    o, st_p = _gla_prompt(q, k, lf, v, g, norm_gain)
    o, st_s = _gla_sample(q, k, lf, v, g, norm_gain, state_all, layer_j, o, st_prev)
    return o, st_p, st_s


def kernel(x_prompt, x_sample, state_ret, state_hgrn, ret_w_in, ret_gn_gain, ret_w_out,
           hgrn_w_in, hgrn_lb_logits, hgrn_norm_gain, hgrn_w_out,
           ln_mix_g, ln_mix_b, ffn_w_in, ffn_w_out, ln_ffn_g, ln_ffn_b):
    assert x_prompt.shape == (BATCH, SEQ, D_MODEL) and x_sample.shape == (DEC_BATCH, DEC_SEQ, D_MODEL)
    assert state_ret.shape == (N_RET, DEC_BATCH, RET_HEADS, RET_DK, RET_DV)
    assert state_hgrn.shape == (N_HGRN, DEC_BATCH, HG_HEADS, HG_DK, HG_DV)
    assert ffn_w_in.shape == (DEPTH, D_MODEL, 2 * D_FF)

    x = jnp.concatenate([x_prompt.reshape(MP_ROWS, D_MODEL), x_sample.reshape(MS_ROWS, D_MODEL)], axis=0)
    xb = x.astype(BF16)
    rope = _rope_tables()
    ret_w_out_b = ret_w_out.astype(BF16)
    hgrn_w_out_b = hgrn_w_out.astype(BF16)
    ffn_w_out_b = ffn_w_out.astype(BF16)

    ret_p, hg_p = [], []
    ret_s = None
    hg_s = None
    for layer in range(DEPTH):
        j = layer // 2
        if layer % 2 == 0:
            o, st_p, ret_s = _retention_layer(xb, ret_w_in, ret_gn_gain[j], state_ret, j, ret_s, rope)
            ret_p.append(st_p)
            w_out = ret_w_out_b
        else:
            o, st_p, hg_s = _hgrn_layer(xb, hgrn_w_in, hgrn_lb_logits, hgrn_norm_gain[j],
                                        state_hgrn, j, hg_s)
            hg_p.append(st_p)
            w_out = hgrn_w_out_b
        x, xb = _dense_ln(o, w_out, j, x, ln_mix_g[layer], ln_mix_b[layer], name="mix_out")
        (h,) = _dense(xb, ffn_w_in, layer, [0, D_FF], D_FF, _ep_swiglu, [BF16], hoist_cast=False,
                      name="ffn_in")
        x, xb = _dense_ln(h, ffn_w_out_b, layer, x, ln_ffn_g[layer], ln_ffn_b[layer], name="ffn_out")

    return (x[:MP_ROWS].reshape(BATCH, SEQ, D_MODEL),
            x[MP_ROWS:].reshape(DEC_BATCH, DEC_SEQ, D_MODEL),
            jnp.stack(ret_p), ret_s, jnp.stack(hg_p), hg_s)
```

```python
import functools
import math

import jax
import jax.numpy as jnp
from jax import lax
from jax.experimental import pallas as pl
from jax.experimental.pallas import tpu as pltpu

F32 = jnp.float32
BF16 = jnp.bfloat16

D_MODEL = 2048
BATCH = 4
SEQ = 2048
DEPTH = 4
DEC_BATCH = 128
DEC_SEQ = 8
PAST_LEN = 16384
N_RET = 2
N_HGRN = 2
RET_HEADS = 8
RET_DK = 256
RET_DV = 512
RET_KD = RET_HEADS * RET_DK
RET_VD = RET_HEADS * RET_DV
HG_HEADS = 16
HG_DK = 128
HG_DV = 128
D_FF = 5632
ROPE_BASE = 10000.0
ALPHA = (2 * DEPTH) ** 0.25
LN_EPS = 1e-5
LB_FLOOR = 1e-30
LOG2_E = 1.4426950408889634

MP_ROWS = BATCH * SEQ
MS_ROWS = DEC_BATCH * DEC_SEQ
M_ROWS = MP_ROWS + MS_ROWS

VMEM_LIMIT_BYTES = 56 * 1024 * 1024

LANES = 128

TM = 1024
TM_TALL = 1536
TN_WIDE = 1024
TN = 512
TM_LN = 1024
LN_ROWS = 256
EPI_COLS = 256
STACK_ROWS = 512

RET_C = 256
RET_HB = 4
GLA_C = 128
GLA_SUB = 8
GLA_HB = 8
SAMPLE_BB = 2
RET_S_HB = 4


def _params(*sem):
    return pltpu.CompilerParams(dimension_semantics=sem, vmem_limit_bytes=VMEM_LIMIT_BYTES)


def _with_optional_alias(body, in_specs, operands, prev, out_index):
    n_in = len(operands)
    if prev is None:
        return (lambda *refs: body(*refs[:n_in], None, *refs[n_in:])), {}
    in_specs.append(pl.BlockSpec(memory_space=pl.ANY))
    operands.append(prev)
    return body, {n_in: out_index}


def _sigmoid(x):
    return 1.0 / (1.0 + jnp.exp(-x))


def _silu(x):
    return x * _sigmoid(x)


def _dense_body(*refs, n_w, n_extra, n_out, epilogue):
    x_ref = refs[0]
    w_refs = refs[1:1 + n_w]
    extra_refs = refs[1 + n_w:1 + n_w + n_extra]
    out_refs = refs[1 + n_w + n_extra:1 + n_w + n_extra + n_out]
    wb_refs = refs[1 + n_w + n_extra + n_out:]

    if wb_refs:
        @pl.when(pl.program_id(1) == 0)
        def _():
            for w_ref, wb_ref in zip(w_refs, wb_refs):
                wb_ref[...] = w_ref[...].astype(BF16)
        rhs = [lambda cols, r=wb_ref: r[:, cols] for wb_ref in wb_refs]
    else:
        rhs = [lambda cols, r=w_ref: r[:, cols].astype(BF16) for w_ref in w_refs]

    x = x_ref[...]
    for c0 in range(0, out_refs[0].shape[1], EPI_COLS):
        cols = slice(c0, c0 + EPI_COLS)
        accs = [jnp.dot(x, get(cols), preferred_element_type=F32) for get in rhs]
        epilogue(accs, extra_refs, out_refs, cols)


def _dense(x, w, layer, col_offs, n_cols, epilogue, out_dtypes, extras=(), extra_specs=(), tn=TN, tm=TM,
           hoist_cast=True, name="dense"):
    m, k = x.shape
    assert m % tm == 0 and n_cols % tn == 0 and all(c % tn == 0 for c in col_offs)
    grid = (n_cols // tn, m // tm)
    in_specs = [pl.BlockSpec((tm, k), lambda j, i: (i, 0))]
    for off in col_offs:
        in_specs.append(pl.BlockSpec((None, k, tn),
                                     functools.partial(lambda j, i, o: (layer, 0, j + o), o=off // tn)))
    in_specs += list(extra_specs)
    out_specs = [pl.BlockSpec((tm, tn), lambda j, i: (i, j)) for _ in out_dtypes]
    out_shape = [jax.ShapeDtypeStruct((m, n_cols), dt) for dt in out_dtypes]
    body = functools.partial(_dense_body, n_w=len(col_offs), n_extra=len(extras),
                             n_out=len(out_dtypes), epilogue=epilogue)
    return pl.pallas_call(
        body, grid=grid, in_specs=in_specs, out_specs=out_specs, out_shape=out_shape,
        scratch_shapes=[pltpu.VMEM((k, tn), BF16) for _ in col_offs] if hoist_cast else [],
        compiler_params=_params("arbitrary", "arbitrary"), name=name,
    )(x, *([w] * len(col_offs)), *extras)


def _ep_plain(accs, extra_refs, out_refs, cols):
    out_refs[0][:, cols] = accs[0].astype(out_refs[0].dtype)


def _ep_silu(accs, extra_refs, out_refs, cols):
    out_refs[0][:, cols] = _silu(accs[0]).astype(out_refs[0].dtype)


def _ep_silu_scaled(accs, extra_refs, out_refs, cols, *, scale):
    out_refs[0][:, cols] = (_silu(accs[0]) * scale).astype(out_refs[0].dtype)


def _ep_swiglu(accs, extra_refs, out_refs, cols):
    out_refs[0][:, cols] = (_silu(accs[0]) * accs[1]).astype(out_refs[0].dtype)


def _ep_rotary(accs, extra_refs, out_refs, cols, *, scale):
    cos_ref, sin_ref = extra_refs
    t = accs[0]
    assert t.shape[1] == RET_DK
    lane = lax.broadcasted_iota(jnp.int32, t.shape, 1)
    swapped = jnp.where((lane & 1) == 0,
                        pltpu.roll(t, t.shape[1] - 1, axis=1),
                        pltpu.roll(t, 1, axis=1))
    r = t * cos_ref[...] + swapped * sin_ref[...]
    if scale != 1.0:
        r = r * scale
    out_refs[0][:, cols] = r.astype(out_refs[0].dtype)


def _ep_forget(accs, extra_refs, out_refs, cols, *, layer):
    logits_ref = extra_refs[0]
    rows = [logits_ref[r:r + 1, cols] for r in range(N_HGRN)]
    mx = functools.reduce(jnp.maximum, rows)
    es = [jnp.exp(r - mx) for r in rows]
    den = functools.reduce(lambda a, b: a + b, es)
    ps = [e / den for e in es]
    lb = functools.reduce(lambda a, b: a + b, ps[:layer + 1]) - ps[0]
    fpre = accs[0]
    t = jnp.exp(-jnp.abs(fpre))
    r = 1.0 / (1.0 + t)
    tr = t * r
    pos = fpre >= 0.0
    one_m_lb = 1.0 - lb
    out_refs[0][:, cols] = jnp.log(jnp.maximum(lb, LB_FLOOR) + one_m_lb * jnp.where(pos, r, tr))
    out_refs[1][:, cols] = one_m_lb * jnp.where(pos, tr, r)


def _dense_ln_body(h_ref, w_ref, x_hbm, g_ref, b_ref, of_ref, ob_ref, x_buf, x_sem, *, n_k):
    i = pl.program_id(0)
    kk = pl.program_id(1)
    tm = of_ref.shape[0]
    x_copy = pltpu.make_async_copy(x_hbm.at[pl.ds(pl.multiple_of(i * tm, tm), tm), :], x_buf, x_sem)

    @pl.when(kk == 0)
    def _():
        x_copy.start()
        of_ref[...] = jnp.dot(h_ref[...], w_ref[...], preferred_element_type=F32)

    @pl.when(kk > 0)
    def _():
        of_ref[...] += jnp.dot(h_ref[...], w_ref[...], preferred_element_type=F32)

    @pl.when(kk == n_k - 1)
    def _():
        x_copy.wait()
        for r0 in range(0, tm, LN_ROWS):
            rows = slice(r0, r0 + LN_ROWS)
            y = of_ref[rows, :] + ALPHA * x_buf[rows, :]
            mu = jnp.mean(y, axis=-1, keepdims=True)
            d = y - mu
            var = jnp.mean(d * d, axis=-1, keepdims=True)
            r = d * lax.rsqrt(var + LN_EPS) * g_ref[...] + b_ref[...]
            of_ref[rows, :] = r
            ob_ref[rows, :] = r.astype(BF16)


def _ln_contraction_tile(k):
    cap = (6 * 1024 * 1024) // (2 * D_MODEL)
    return max(t for t in range(LANES, cap + 1, LANES) if k % t == 0)


def _dense_ln(h, w_bf16, layer, x_res, gain, bias, name):
    m, k = h.shape
    d = w_bf16.shape[2]
    tk = _ln_contraction_tile(k)
    assert m % TM_LN == 0 and k % tk == 0
    n_k = k // tk
    return pl.pallas_call(
        functools.partial(_dense_ln_body, n_k=n_k),
        grid=(m // TM_LN, n_k),
        in_specs=[pl.BlockSpec((TM_LN, tk), lambda i, kk: (i, kk)),
                  pl.BlockSpec((None, tk, d), lambda i, kk: (layer, kk, 0)),
                  pl.BlockSpec(memory_space=pl.ANY),
                  pl.BlockSpec((1, d), lambda i, kk: (0, 0)),
                  pl.BlockSpec((1, d), lambda i, kk: (0, 0))],
        out_specs=[pl.BlockSpec((TM_LN, d), lambda i, kk: (i, 0)),
                   pl.BlockSpec((TM_LN, d), lambda i, kk: (i, 0))],
        out_shape=[jax.ShapeDtypeStruct((m, d), F32), jax.ShapeDtypeStruct((m, d), BF16)],
        scratch_shapes=[pltpu.VMEM((TM_LN, d), F32), pltpu.SemaphoreType.DMA(())],
        compiler_params=_params("arbitrary", "arbitrary"), name=name,
    )(h, w_bf16, x_res, gain.reshape(1, d), bias.reshape(1, d))


def _group_norm_gate(o, gain, gate):
    mu = jnp.mean(o, axis=-1, keepdims=True)
    d = o - mu
    var = jnp.mean(d * d, axis=-1, keepdims=True)
    return gate * (d * lax.rsqrt(var + LN_EPS) * gain)


def _ret_prompt_body(q_ref, k_ref, v_ref, g_ref, gain_ref, dmat_ref, qdec_ref, kdec_ref, cdec_ref,
                     st_prev_ref, o_ref, s_ref):
    del st_prev_ref

    @pl.when(pl.program_id(2) == 0)
    def _():
        s_ref[...] = jnp.zeros_like(s_ref)

    heads = range(RET_HB)
    ks = [slice(h * RET_DK, (h + 1) * RET_DK) for h in heads]
    vs = [slice(h * RET_DV, (h + 1) * RET_DV) for h in heads]
    q = q_ref[...]
    k = k_ref[...]
    qb = q.astype(BF16)
    kb = k.astype(BF16)
    qd = (q * qdec_ref[...]).astype(BF16)
    kd = (k * kdec_ref[...]).astype(BF16)
    a = [(lax.dot_general(qb[:, ks[h]], kb[:, ks[h]], (((1,), (1,)), ((), ())),
                          preferred_element_type=F32) * dmat_ref[h]).astype(BF16) for h in heads]
    s_old = [s_ref[h] for h in heads]
    o = [jnp.dot(a[h], v_ref[:, vs[h]], preferred_element_type=F32)
         + jnp.dot(qd[:, ks[h]], s_old[h].astype(BF16), preferred_element_type=F32) for h in heads]
    for h in heads:
        s_ref[h] = cdec_ref[:, vs[h]] * s_old[h] + lax.dot_general(
            kd[:, ks[h]], v_ref[:, vs[h]], (((0,), (0,)), ((), ())), preferred_element_type=F32)
    o_ref[...] = jnp.concatenate(
        [_group_norm_gate(o[h], gain_ref[:, vs[h]], g_ref[:, vs[h]]) for h in heads], axis=1).astype(BF16)


def _ret_decay_tables(c):
    lg = jnp.log(1.0 - 2.0 ** (-5.0 - jnp.arange(RET_HEADS, dtype=F32)))
    idx = jnp.arange(c, dtype=F32)
    rel = idx[:, None] - idx[None, :]
    causal = rel >= 0
    dmat = jnp.where(causal[None], jnp.exp(jnp.where(causal, rel, 0.0)[None] * lg[:, None, None]), 0.0)
    qdec = jnp.exp((idx + 1)[None, :] * lg[:, None])
    kdec = jnp.exp((c - 1 - idx)[None, :] * lg[:, None])
    cdec = jnp.exp(c * lg)
    qdec = jnp.broadcast_to(qdec[:, :, None], (RET_HEADS, c, RET_DK))
    kdec = jnp.broadcast_to(kdec[:, :, None], (RET_HEADS, c, RET_DK))
    cdec = jnp.broadcast_to(cdec[:, None, None], (RET_HEADS, 1, RET_DV))
    return dmat, qdec, kdec, cdec


def _ret_prompt(q, k, v, g, gain, layer_j, st_prev):
    c = RET_C
    nc = SEQ // c
    dmat, qdec, kdec, cdec = _ret_decay_tables(c)
    qdec = jnp.transpose(qdec, (1, 0, 2)).reshape(c, RET_KD)
    kdec = jnp.transpose(kdec, (1, 0, 2)).reshape(c, RET_KD)
    cdec = jnp.transpose(cdec, (1, 0, 2)).reshape(1, RET_VD)
    kw = RET_HB * RET_DK
    vw = RET_HB * RET_DV
    row = lambda b, h, ci: (b * nc + ci, h)
    lane = lambda b, h, ci: (0, h)
    in_specs = [pl.BlockSpec((c, kw), row), pl.BlockSpec((c, kw), row),
                pl.BlockSpec((c, vw), row), pl.BlockSpec((c, vw), row),
                pl.BlockSpec((1, vw), lane),
                pl.BlockSpec((RET_HB, c, c), lambda b, h, ci: (h, 0, 0)), pl.BlockSpec((c, kw), lane),
                pl.BlockSpec((c, kw), lane), pl.BlockSpec((1, vw), lane)]
    operands = [q, k, v, g, gain.reshape(1, RET_VD), dmat, qdec, kdec, cdec]
    body, aliases = _with_optional_alias(_ret_prompt_body, in_specs, operands, st_prev, out_index=1)
    return pl.pallas_call(
        body,
        grid=(BATCH, RET_HEADS // RET_HB, nc),
        in_specs=in_specs,
        out_specs=[pl.BlockSpec((c, vw), row),
                   pl.BlockSpec((None, None, RET_HB, RET_DK, RET_DV),
                                lambda b, h, ci: (layer_j, b, h, 0, 0))],
        out_shape=[jax.ShapeDtypeStruct((M_ROWS, RET_VD), BF16),
                   jax.ShapeDtypeStruct((N_RET, BATCH, RET_HEADS, RET_DK, RET_DV), F32)],
        input_output_aliases=aliases,
        compiler_params=_params("parallel", "parallel", "arbitrary"), name="ret_prompt",
    )(*operands)


def _ret_sample_body(q_ref, k_ref, v_ref, g_ref, gain_ref, dmat_ref, qdec_ref, kdec_ref, cdec_ref,
                     s0_ref, o_prev_ref, st_prev_ref, o_ref, s_ref):
    del o_prev_ref, st_prev_ref
    units = [(bi, h) for bi in range(SAMPLE_BB) for h in range(RET_S_HB)]
    rows = lambda bi: slice(bi * DEC_SEQ, (bi + 1) * DEC_SEQ)
    ks = lambda h: slice(h * RET_DK, (h + 1) * RET_DK)
    vs = lambda h: slice(h * RET_DV, (h + 1) * RET_DV)
    v_all = v_ref[...].astype(F32)
    q = [q_ref[rows(bi), ks(h)] for bi, h in units]
    k = [k_ref[rows(bi), ks(h)] for bi, h in units]
    v = [v_all[rows(bi), vs(h)].astype(BF16) for bi, h in units]
    a = [(lax.dot_general(q[u].astype(BF16), k[u].astype(BF16), (((1,), (1,)), ((), ())),
                          preferred_element_type=F32) * dmat_ref[h]).astype(BF16)
         for u, (bi, h) in enumerate(units)]
    o = [jnp.dot(a[u], v[u], preferred_element_type=F32)
         + jnp.dot((q[u] * qdec_ref[h]).astype(BF16), s0_ref[0, bi, h].astype(BF16),
                   preferred_element_type=F32) for u, (bi, h) in enumerate(units)]
    for u, (bi, h) in enumerate(units):
        s_ref[0, bi, h] = cdec_ref[h] * s0_ref[0, bi, h] + lax.dot_general(
            (k[u] * kdec_ref[h]).astype(BF16), v[u], (((0,), (0,)), ((), ())), preferred_element_type=F32)
    gated = [_group_norm_gate(o[u], gain_ref[:, vs(h)], g_ref[rows(bi), vs(h)])
             for u, (bi, h) in enumerate(units)]
    o_ref[...] = jnp.concatenate(
        [jnp.concatenate(gated[bi * RET_S_HB:(bi + 1) * RET_S_HB], axis=1) for bi in range(SAMPLE_BB)],
        axis=0).astype(BF16)


def _ret_sample(q, k, v, g, gain, state_all, layer_j, o_prev, st_prev):
    c = DEC_SEQ
    dmat, qdec, kdec, cdec = _ret_decay_tables(c)
    rows_per = SAMPLE_BB * DEC_SEQ
    row0 = MP_ROWS // rows_per
    n_hb = RET_HEADS // RET_S_HB
    row = lambda bb, hb: (row0 + bb, hb)
    hsel = lambda bb, hb: (hb, 0, 0)
    st_block = (1, SAMPLE_BB, RET_S_HB, RET_DK, RET_DV)
    st_map = lambda bb, hb: (layer_j, bb, hb, 0, 0)
    in_specs = [pl.BlockSpec((rows_per, RET_S_HB * RET_DK), row),
                pl.BlockSpec((rows_per, RET_S_HB * RET_DK), row),
                pl.BlockSpec((rows_per, RET_S_HB * RET_DV), row),
                pl.BlockSpec((rows_per, RET_S_HB * RET_DV), row),
                pl.BlockSpec((1, RET_S_HB * RET_DV), lambda bb, hb: (0, hb)),
                pl.BlockSpec((RET_S_HB, c, c), hsel),
                pl.BlockSpec((RET_S_HB, c, RET_DK), hsel),
                pl.BlockSpec((RET_S_HB, c, RET_DK), hsel),
                pl.BlockSpec((RET_S_HB, 1, RET_DV), hsel),
                pl.BlockSpec(st_block, st_map),
                pl.BlockSpec(memory_space=pl.ANY)]
    operands = [q, k, v, g, gain.reshape(1, RET_VD), dmat, qdec, kdec, cdec, state_all, o_prev]
    aliases = {10: 0}
    if st_prev is not None:
        in_specs.append(pl.BlockSpec(memory_space=pl.ANY))
        operands.append(st_prev)
        aliases[11] = 1
        body = _ret_sample_body
    else:
        body = lambda *refs: _ret_sample_body(*refs[:11], None, *refs[11:])
    return pl.pallas_call(
        body,
        grid=(DEC_BATCH // SAMPLE_BB, n_hb),
        in_specs=in_specs,
        out_specs=[pl.BlockSpec((rows_per, RET_S_HB * RET_DV), row), pl.BlockSpec(st_block, st_map)],
        out_shape=[jax.ShapeDtypeStruct((M_ROWS, RET_VD), BF16),
                   jax.ShapeDtypeStruct((N_RET, DEC_BATCH, RET_HEADS, RET_DK, RET_DV), F32)],
        input_output_aliases=aliases,
        compiler_params=_params("parallel", "parallel"), name="ret_sample",
    )(*operands)


def _rms_norm_gate(o, gain, gate):
    return o * lax.rsqrt(jnp.mean(o * o, axis=-1, keepdims=True) + LN_EPS) * gain * gate


def _diag_tiles(g2_blk, q_blk, g2_row, k_row):
    n = g2_blk.shape[0]
    return [jnp.exp2(g2_blk - g2_row(j)) * q_blk * k_row(j) for j in range(n)]


def _select_diag(sums, n, lane_base):
    lanes = sums.shape[1]
    rel = lax.broadcasted_iota(jnp.int32, (n, lanes), 1) - lane_base
    row = lax.broadcasted_iota(jnp.int32, (n, lanes), 0)
    acc = jnp.zeros((n, lanes), F32)
    for j in range(n):
        acc = jnp.where(rel == j, sums[j * n:(j + 1) * n, :], acc)
    return jnp.where(row >= rel, acc, 0.0)


def _gla_prompt_body(q_ref, k_ref, lf_ref, v_ref, g_ref, gain_ref, ltri_ref, st_prev_ref,
                     o_ref, st_ref, s_scr, g_scr):
    del st_prev_ref
    ci = pl.program_id(2)
    c = GLA_C
    n = GLA_SUB

    @pl.when(ci == 0)
    def _():
        s_scr[...] = jnp.zeros_like(s_scr)

    width = q_ref.shape[1]
    heads = [slice(hh * HG_DK, (hh + 1) * HG_DK) for hh in range(GLA_HB)]
    ones_b = jnp.ones((HG_DK, c), BF16)
    row = lax.broadcasted_iota(jnp.int32, (c, width), 0)
    ri = lax.broadcasted_iota(jnp.int32, (c, c), 0)
    cj = lax.broadcasted_iota(jnp.int32, (c, c), 1)

    g2 = LOG2_E * jnp.dot(ltri_ref[...], lf_ref[...], precision=lax.Precision.HIGHEST,
                          preferred_element_type=F32)
    g_scr[...] = g2
    q = q_ref[...]
    k = k_ref[...]

    a = [jnp.zeros((c, c), F32) for _ in heads]
    s = c // 2
    while s >= n:
        ref_rows = [jnp.broadcast_to(g_scr[m + s - 1:m + s, :], (2 * s, width)) for m in range(0, c, 2 * s)]
        ref = ref_rows[0] if len(ref_rows) == 1 else jnp.concatenate(ref_rows, axis=0)
        e = jnp.exp2(-jnp.abs(g2 - ref))
        second = (row & s) != 0
        qt = jnp.where(second, q * e, 0.0).astype(BF16)
        kt = jnp.where(second, 0.0, k * e).astype(BF16)
        for hh, sl in enumerate(heads):
            p = lax.dot_general(qt[:, sl], kt[:, sl], (((1,), (1,)), ((), ())), preferred_element_type=F32)
            if 2 * s < c:
                sh = int(math.log2(2 * s))
                p = jnp.where((ri >> sh) == (cj >> sh), p, 0.0)
            a[hh] = a[hh] + p
        s //= 2

    tiles = []
    for sl in heads:
        for m in range(0, c, n):
            tiles += _diag_tiles(g2[m:m + n, sl], q[m:m + n, sl],
                                 lambda j, m=m, sl=sl: g_scr[m + j:m + j + 1, sl],
                                 lambda j, m=m, sl=sl: k_ref[m + j:m + j + 1, sl])
    sums = jnp.dot(jnp.concatenate(tiles, axis=0).astype(BF16), ones_b, preferred_element_type=F32)

    qe = (q * jnp.exp2(g2)).astype(BF16)
    g2_last = g_scr[c - 1:c, :]
    kd = (k * jnp.exp2(g2_last - g2)).astype(BF16)
    decay = jnp.exp2(g2_last)
    outs = []
    for hh, sl in enumerate(heads):
        base = hh * c * n
        diag = [_select_diag(sums[base + m * n:base + (m + n) * n, :], n, m) for m in range(0, c, n)]
        a_h = (a[hh] + jnp.concatenate(diag, axis=0)).astype(BF16)
        v = v_ref[:, sl]
        s_t = s_scr[hh]
        o = jnp.dot(a_h, v, preferred_element_type=F32)
        o = o + lax.dot_general(qe[:, sl], s_t.astype(BF16), (((1,), (1,)), ((), ())),
                                preferred_element_type=F32)
        s_scr[hh] = decay[:, sl] * s_t + lax.dot_general(
            v, kd[:, sl], (((0,), (0,)), ((), ())), preferred_element_type=F32)
        outs.append(_rms_norm_gate(o, gain_ref[...], g_ref[:, sl]))

    o_ref[...] = jnp.concatenate(outs, axis=1).astype(BF16)

    @pl.when(ci == pl.num_programs(2) - 1)
    def _():
        for hh in range(GLA_HB):
            st_ref[hh] = s_scr[hh].T


def _gla_prompt(q, k, lf, v, g, gain, layer_j, st_prev):
    c = GLA_C
    nc = SEQ // c
    w = GLA_HB * HG_DK
    ltri = (jnp.arange(c)[:, None] >= jnp.arange(c)[None, :]).astype(F32)
    row = lambda b, hb, ci: (b * nc + ci, hb)
    in_specs = [pl.BlockSpec((c, w), row), pl.BlockSpec((c, w), row), pl.BlockSpec((c, w), row),
                pl.BlockSpec((c, w), row), pl.BlockSpec((c, w), row),
                pl.BlockSpec((1, HG_DV), lambda b, hb, ci: (0, 0)),
                pl.BlockSpec((c, c), lambda b, hb, ci: (0, 0))]
    operands = [q, k, lf, v, g, gain.reshape(1, HG_DV), ltri]
    body, aliases = _with_optional_alias(_gla_prompt_body, in_specs, operands, st_prev, out_index=1)
    return pl.pallas_call(
        body,
        grid=(BATCH, HG_HEADS // GLA_HB, nc),
        in_specs=in_specs,
        out_specs=[pl.BlockSpec((c, w), row),
                   pl.BlockSpec((None, None, GLA_HB, HG_DK, HG_DV),
                                lambda b, hb, ci: (layer_j, b, hb, 0, 0))],
        out_shape=[jax.ShapeDtypeStruct((M_ROWS, D_MODEL), BF16),
                   jax.ShapeDtypeStruct((N_HGRN, BATCH, HG_HEADS, HG_DK, HG_DV), F32)],
        input_output_aliases=aliases,
        scratch_shapes=[pltpu.VMEM((GLA_HB, HG_DV, HG_DK), F32), pltpu.VMEM((c, w), F32)],
        compiler_params=_params("parallel", "parallel", "arbitrary"), name="gla_prompt",
    )(*operands)


def _gla_sample_body(q_ref, k_ref, lf_ref, v_ref, g_ref, gain_ref, s0_ref, o_prev_ref, st_prev_ref,
                     o_ref, s_ref, g_scr):
    del o_prev_ref, st_prev_ref
    n = DEC_SEQ
    rows_per = SAMPLE_BB * n
    width = q_ref.shape[1]
    rix = lax.broadcasted_iota(jnp.int32, (rows_per, width), 0) & (n - 1)
    g2_all = lf_ref[...]
    sh = 1
    while sh < n:
        g2_all = g2_all + jnp.where(rix >= sh, pltpu.roll(g2_all, sh, axis=0), 0.0)
        sh *= 2
    g2_all = LOG2_E * g2_all
    g_scr[...] = g2_all
    q_all = q_ref[...]
    g2_last = jnp.concatenate([jnp.broadcast_to(g_scr[(bi + 1) * n - 1:(bi + 1) * n, :], (n, width))
                               for bi in range(SAMPLE_BB)], axis=0)
    qe_all = q_all * jnp.exp2(g2_all)
    kd_all = k_ref[...] * jnp.exp2(g2_last - g2_all)
    decay_all = jnp.exp2(g2_last)
    v_all = v_ref[...].astype(F32)
    ones_b = jnp.ones((HG_DK, HG_DK), BF16)
    zpad = jnp.zeros((HG_DK - n, HG_DV), F32)
    units = [(bi, h) for bi in range(SAMPLE_BB) for h in range(HG_HEADS)]
    span = lambda bi, h: (slice(bi * n, (bi + 1) * n), slice(h * HG_DK, (h + 1) * HG_DK))

    tiles = []
    for bi, h in units:
        rows, sl = span(bi, h)
        tiles += _diag_tiles(g2_all[rows, sl], q_all[rows, sl],
                             lambda j, bi=bi, sl=sl: g_scr[bi * n + j:bi * n + j + 1, sl],
                             lambda j, bi=bi, sl=sl: k_ref[bi * n + j:bi * n + j + 1, sl])
    sums = jnp.dot(jnp.concatenate(tiles, axis=0).astype(BF16), ones_b, preferred_element_type=F32)

    v_pads, o_units = [], []
    for u, (bi, h) in enumerate(units):
        rows, sl = span(bi, h)
        a = _select_diag(sums[u * n * n:(u + 1) * n * n, :], n, 0)
        v_pad = jnp.concatenate([v_all[rows, sl], zpad], axis=0).astype(BF16)
        v_pads.append(v_pad)
        o = jnp.dot(a.astype(BF16), v_pad, preferred_element_type=F32)
        o_units.append(o + jnp.dot(qe_all[rows, sl].astype(BF16), s0_ref[0, bi, h].astype(BF16),
                                   preferred_element_type=F32))
    for u, (bi, h) in enumerate(units):
        rows, sl = span(bi, h)
        kd_pad = jnp.concatenate([kd_all[rows, sl], zpad], axis=0).astype(BF16)
        decay_col = jnp.broadcast_to(decay_all[bi * n:bi * n + 1, sl], (HG_DV, HG_DK)).T
        s_ref[0, bi, h] = decay_col * s0_ref[0, bi, h] + lax.dot_general(
            kd_pad, v_pads[u], (((0,), (0,)), ((), ())), preferred_element_type=F32)
    outs = []
    for bi in range(SAMPLE_BB):
        per_head = [_rms_norm_gate(o_units[bi * HG_HEADS + h], gain_ref[...], g_ref[span(bi, h)])
                    for h in range(HG_HEADS)]
        outs.append(jnp.concatenate(per_head, axis=1))
    o_ref[...] = jnp.concatenate(outs, axis=0).astype(BF16)


def _gla_sample(q, k, lf, v, g, gain, state_all, layer_j, o_prev, st_prev):
    rows_per = SAMPLE_BB * DEC_SEQ
    row0 = MP_ROWS // rows_per
    row = lambda bb: (row0 + bb, 0)
    st_block = (1, SAMPLE_BB, HG_HEADS, HG_DK, HG_DV)
    st_map = lambda bb: (layer_j, bb, 0, 0, 0)
    in_specs = [pl.BlockSpec((rows_per, D_MODEL), row)] * 5 + [
        pl.BlockSpec((1, HG_DV), lambda bb: (0, 0)),
        pl.BlockSpec(st_block, st_map),
        pl.BlockSpec(memory_space=pl.ANY)]
    operands = [q, k, lf, v, g, gain.reshape(1, HG_DV), state_all, o_prev]
    aliases = {7: 0}
    if st_prev is not None:
        in_specs.append(pl.BlockSpec(memory_space=pl.ANY))
        operands.append(st_prev)
        aliases[8] = 1
        body = _gla_sample_body
    else:
        body = lambda *refs: _gla_sample_body(*refs[:8], None, *refs[8:])
    return pl.pallas_call(
        body,
        grid=(DEC_BATCH // SAMPLE_BB,),
        in_specs=in_specs,
        out_specs=[pl.BlockSpec((rows_per, D_MODEL), row), pl.BlockSpec(st_block, st_map)],
        out_shape=[jax.ShapeDtypeStruct((M_ROWS, D_MODEL), BF16),
                   jax.ShapeDtypeStruct((N_HGRN, DEC_BATCH, HG_HEADS, HG_DK, HG_DV), F32)],
        input_output_aliases=aliases,
        scratch_shapes=[pltpu.VMEM((rows_per, D_MODEL), F32)],
        compiler_params=_params("parallel"), name="gla_sample",
    )(*operands)


def _stack_rows_body(xp_ref, xs_ref, of_ref, ob_ref, *, n_prompt_blocks):
    i = pl.program_id(0)

    @pl.when(i < n_prompt_blocks)
    def _():
        of_ref[...] = xp_ref[...]
        ob_ref[...] = xp_ref[...].astype(BF16)

    @pl.when(i >= n_prompt_blocks)
    def _():
        of_ref[...] = xs_ref[...]
        ob_ref[...] = xs_ref[...].astype(BF16)


def _stack_rows(xp, xs):
    rows = STACK_ROWS
    n_p, n_s = xp.shape[0] // rows, xs.shape[0] // rows
    assert xp.shape[0] % rows == 0 and xs.shape[0] % rows == 0
    d = xp.shape[1]
    return pl.pallas_call(
        functools.partial(_stack_rows_body, n_prompt_blocks=n_p),
        grid=(n_p + n_s,),
        in_specs=[pl.BlockSpec((rows, d), lambda i: (jnp.minimum(i, n_p - 1), 0)),
                  pl.BlockSpec((rows, d), lambda i: (jnp.maximum(i - n_p, 0), 0))],
        out_specs=[pl.BlockSpec((rows, d), lambda i: (i, 0)), pl.BlockSpec((rows, d), lambda i: (i, 0))],
        out_shape=[jax.ShapeDtypeStruct((n_p * rows + n_s * rows, d), F32),
                   jax.ShapeDtypeStruct((n_p * rows + n_s * rows, d), BF16)],
        compiler_params=_params("arbitrary"), name="stack_rows",
    )(xp, xs)


def _rope_tables():
    inv = 1.0 / (ROPE_BASE ** jnp.linspace(0.0, 1.0, RET_DK // 2, dtype=F32))
    pos = jnp.concatenate([jnp.tile(0 + jnp.arange(SEQ, dtype=F32), BATCH),
                           jnp.tile(PAST_LEN + jnp.arange(DEC_SEQ, dtype=F32), DEC_BATCH)])
    ang = pos[:, None] * inv[None, :]
    cos = jnp.repeat(jnp.cos(ang), 2, axis=-1)
    sin = jnp.repeat(jnp.sin(ang), 2, axis=-1)
    sign = jnp.tile(jnp.array([-1.0, 1.0], F32), RET_DK // 2)
    return cos, sin * sign[None, :]


def _retention_layer(xb, w_in, gn_gain, state_all, layer_j, st_prev, rope):
    cos, sin = rope
    rope_specs = [pl.BlockSpec((TM, RET_DK), lambda j, i: (i, 0))] * 2
    (q,) = _dense(xb, w_in, layer_j, [0], RET_KD, functools.partial(_ep_rotary, scale=1.0), [F32],
                  extras=(cos, sin), extra_specs=rope_specs, tn=TN_WIDE, name="ret_q")
    (k,) = _dense(xb, w_in, layer_j, [RET_KD], RET_KD,
                  functools.partial(_ep_rotary, scale=RET_DK ** -0.5), [F32],
                  extras=(cos, sin), extra_specs=rope_specs, tn=TN_WIDE, name="ret_k")
    (v,) = _dense(xb, w_in, layer_j, [2 * RET_KD], RET_VD, _ep_plain, [BF16], tn=TN_WIDE, name="ret_v")
    (g,) = _dense(xb, w_in, layer_j, [2 * RET_KD + RET_VD], RET_VD, _ep_silu, [F32], tn=TN_WIDE,
                  name="ret_g")
    o, st_p = _ret_prompt(q, k, v, g, gn_gain, layer_j, st_prev[0])
    o, st_s = _ret_sample(q, k, v, g, gn_gain, state_all, layer_j, o, st_prev[1])
    return o, (st_p, st_s)


def _hgrn_layer(xb, w_in, lb_logits, norm_gain, state_all, layer_j, st_prev):
    d = D_MODEL
    logit_spec = [pl.BlockSpec((N_HGRN, TN_WIDE), lambda j, i: (0, j))]
    (q,) = _dense(xb, w_in, layer_j, [0], d, functools.partial(_ep_silu_scaled, scale=HG_DK ** -0.5),
                  [F32], tn=TN_WIDE, name="hg_q")
    lf, k = _dense(xb, w_in, layer_j, [d], d, functools.partial(_ep_forget, layer=layer_j), [F32, F32],
                   extras=(lb_logits,), extra_specs=logit_spec, tn=TN_WIDE, name="hg_f")
    (v,) = _dense(xb, w_in, layer_j, [2 * d], d, _ep_plain, [BF16], tn=TN_WIDE, name="hg_v")
    (g,) = _dense(xb, w_in, layer_j, [3 * d], d, _ep_silu, [F32], tn=TN_WIDE, name="hg_g")
    o, st_p = _gla_prompt(q, k, lf, v, g, norm_gain, layer_j, st_prev[0])
    o, st_s = _gla_sample(q, k, lf, v, g, norm_gain, state_all, layer_j, o, st_prev[1])
    return o, (st_p, st_s)


def kernel(x_prompt, x_sample, state_ret, state_hgrn, ret_w_in, ret_gn_gain, ret_w_out,
           hgrn_w_in, hgrn_lb_logits, hgrn_norm_gain, hgrn_w_out,
           ln_mix_g, ln_mix_b, ffn_w_in, ffn_w_out, ln_ffn_g, ln_ffn_b):
    assert x_prompt.shape == (BATCH, SEQ, D_MODEL) and x_sample.shape == (DEC_BATCH, DEC_SEQ, D_MODEL)
    assert state_ret.shape == (N_RET, DEC_BATCH, RET_HEADS, RET_DK, RET_DV)
    assert state_hgrn.shape == (N_HGRN, DEC_BATCH, HG_HEADS, HG_DK, HG_DV)
    assert ffn_w_in.shape == (DEPTH, D_MODEL, 2 * D_FF)

    x, xb = _stack_rows(x_prompt.reshape(MP_ROWS, D_MODEL), x_sample.reshape(MS_ROWS, D_MODEL))
    rope = _rope_tables()
    ret_w_out_b = ret_w_out.astype(BF16)
    hgrn_w_out_b = hgrn_w_out.astype(BF16)
    ffn_w_out_b = ffn_w_out.astype(BF16)

    ret_st = (None, None)
    hg_st = (None, None)
    for layer in range(DEPTH):
        j = layer // 2
        if layer % 2 == 0:
            o, ret_st = _retention_layer(xb, ret_w_in, ret_gn_gain[j], state_ret, j, ret_st, rope)
            w_out = ret_w_out_b
        else:
            o, hg_st = _hgrn_layer(xb, hgrn_w_in, hgrn_lb_logits, hgrn_norm_gain[j], state_hgrn, j, hg_st)
            w_out = hgrn_w_out_b
        x, xb = _dense_ln(o, w_out, j, x, ln_mix_g[layer], ln_mix_b[layer], name="mix_out")
        (h,) = _dense(xb, ffn_w_in, layer, [0, D_FF], D_FF, _ep_swiglu, [BF16], tm=TM_TALL,
                      hoist_cast=False, name="ffn_in")
        x, xb = _dense_ln(h, ffn_w_out_b, layer, x, ln_ffn_g[layer], ln_ffn_b[layer], name="ffn_out")

    return (x[:MP_ROWS].reshape(BATCH, SEQ, D_MODEL),
            x[MP_ROWS:].reshape(DEC_BATCH, DEC_SEQ, D_MODEL),
            ret_st[0], ret_st[1], hg_st[0], hg_st[1])
```

```python
import functools
import math

import jax
import jax.numpy as jnp
from jax import lax
from jax.experimental import pallas as pl
from jax.experimental.pallas import tpu as pltpu

F32 = jnp.float32
BF16 = jnp.bfloat16

D_MODEL = 2048
BATCH = 4
SEQ = 2048
DEPTH = 4
DEC_BATCH = 128
DEC_SEQ = 8
PAST_LEN = 16384
N_RET = 2
N_HGRN = 2
RET_HEADS = 8
RET_DK = 256
RET_DV = 512
RET_KD = RET_HEADS * RET_DK
RET_VD = RET_HEADS * RET_DV
HG_HEADS = 16
HG_DK = 128
HG_DV = 128
D_FF = 5632
ROPE_BASE = 10000.0
ALPHA = (2 * DEPTH) ** 0.25
LN_EPS = 1e-5
LB_FLOOR = 1e-30
LOG2_E = 1.4426950408889634

MP_ROWS = BATCH * SEQ
MS_ROWS = DEC_BATCH * DEC_SEQ
M_ROWS = MP_ROWS + MS_ROWS

VMEM_LIMIT_BYTES = 56 * 1024 * 1024

LANES = 128

TM = 1024
TM_TALL = 1536
TN_WIDE = 1024
TN = 512
TM_LN = 1024
LN_ROWS = 256
EPI_COLS = 256
STACK_ROWS = 512

RET_C = 256
RET_HB = 4
GLA_C = 128
GLA_SUB = 8
GLA_HB = 8
SAMPLE_BB = 2
RET_S_BB = 4
RET_S_HB = 4


def _params(*sem):
    return pltpu.CompilerParams(dimension_semantics=sem, vmem_limit_bytes=VMEM_LIMIT_BYTES)


def _with_optional_alias(body, in_specs, operands, prev, out_index):
    n_in = len(operands)
    if prev is None:
        return (lambda *refs: body(*refs[:n_in], None, *refs[n_in:])), {}
    in_specs.append(pl.BlockSpec(memory_space=pl.ANY))
    operands.append(prev)
    return body, {n_in: out_index}


def _sigmoid(x):
    return 1.0 / (1.0 + jnp.exp(-x))


def _silu(x):
    return x * _sigmoid(x)


def _dense_body(*refs, n_w, n_extra, n_out, epilogue):
    x_ref = refs[0]
    w_refs = refs[1:1 + n_w]
    extra_refs = refs[1 + n_w:1 + n_w + n_extra]
    out_refs = refs[1 + n_w + n_extra:1 + n_w + n_extra + n_out]
    wb_refs = refs[1 + n_w + n_extra + n_out:]

    if wb_refs:
        @pl.when(pl.program_id(1) == 0)
        def _():
            for w_ref, wb_ref in zip(w_refs, wb_refs):
                wb_ref[...] = w_ref[...].astype(BF16)
        rhs = [lambda cols, r=wb_ref: r[:, cols] for wb_ref in wb_refs]
    else:
        rhs = [lambda cols, r=w_ref: r[:, cols].astype(BF16) for w_ref in w_refs]

    x = x_ref[...]
    for c0 in range(0, out_refs[0].shape[1], EPI_COLS):
        cols = slice(c0, c0 + EPI_COLS)
        accs = [jnp.dot(x, get(cols), preferred_element_type=F32) for get in rhs]
        epilogue(accs, extra_refs, out_refs, cols)


def _dense(x, w, layer, col_offs, n_cols, epilogue, out_dtypes, extras=(), extra_specs=(), tn=TN, tm=TM,
           hoist_cast=True, name="dense"):
    m, k = x.shape
    assert m % tm == 0 and n_cols % tn == 0 and all(c % tn == 0 for c in col_offs)
    grid = (n_cols // tn, m // tm)
    in_specs = [pl.BlockSpec((tm, k), lambda j, i: (i, 0))]
    for off in col_offs:
        in_specs.append(pl.BlockSpec((None, k, tn),
                                     functools.partial(lambda j, i, o: (layer, 0, j + o), o=off // tn)))
    in_specs += list(extra_specs)
    out_specs = [pl.BlockSpec((tm, tn), lambda j, i: (i, j)) for _ in out_dtypes]
    out_shape = [jax.ShapeDtypeStruct((m, n_cols), dt) for dt in out_dtypes]
    body = functools.partial(_dense_body, n_w=len(col_offs), n_extra=len(extras),
                             n_out=len(out_dtypes), epilogue=epilogue)
    return pl.pallas_call(
        body, grid=grid, in_specs=in_specs, out_specs=out_specs, out_shape=out_shape,
        scratch_shapes=[pltpu.VMEM((k, tn), BF16) for _ in col_offs] if hoist_cast else [],
        compiler_params=_params("arbitrary", "arbitrary"), name=name,
    )(x, *([w] * len(col_offs)), *extras)


def _ep_plain(accs, extra_refs, out_refs, cols):
    out_refs[0][:, cols] = accs[0].astype(out_refs[0].dtype)


def _ep_silu(accs, extra_refs, out_refs, cols):
    out_refs[0][:, cols] = _silu(accs[0]).astype(out_refs[0].dtype)


def _ep_silu_scaled(accs, extra_refs, out_refs, cols, *, scale):
    out_refs[0][:, cols] = (_silu(accs[0]) * scale).astype(out_refs[0].dtype)


def _ep_swiglu(accs, extra_refs, out_refs, cols):
    out_refs[0][:, cols] = (_silu(accs[0]) * accs[1]).astype(out_refs[0].dtype)


def _ep_rotary(accs, extra_refs, out_refs, cols, *, scale):
    cos_ref, sin_ref = extra_refs
    t = accs[0]
    assert t.shape[1] == RET_DK
    lane = lax.broadcasted_iota(jnp.int32, t.shape, 1)
    swapped = jnp.where((lane & 1) == 0,
                        pltpu.roll(t, t.shape[1] - 1, axis=1),
                        pltpu.roll(t, 1, axis=1))
    r = t * cos_ref[...] + swapped * sin_ref[...]
    if scale != 1.0:
        r = r * scale
    out_refs[0][:, cols] = r.astype(out_refs[0].dtype)


def _ep_forget(accs, extra_refs, out_refs, cols, *, layer):
    logits_ref = extra_refs[0]
    rows = [logits_ref[r:r + 1, cols] for r in range(N_HGRN)]
    mx = functools.reduce(jnp.maximum, rows)
    es = [jnp.exp(r - mx) for r in rows]
    den = functools.reduce(lambda a, b: a + b, es)
    ps = [e / den for e in es]
    lb = functools.reduce(lambda a, b: a + b, ps[:layer + 1]) - ps[0]
    fpre = accs[0]
    t = jnp.exp(-jnp.abs(fpre))
    r = 1.0 / (1.0 + t)
    tr = t * r
    pos = fpre >= 0.0
    one_m_lb = 1.0 - lb
    out_refs[0][:, cols] = jnp.log(jnp.maximum(lb, LB_FLOOR) + one_m_lb * jnp.where(pos, r, tr))
    out_refs[1][:, cols] = one_m_lb * jnp.where(pos, tr, r)


def _dense_ln_body(h_ref, w_ref, x_hbm, g_ref, b_ref, of_ref, ob_ref, x_buf, x_sem, *, n_k):
    i = pl.program_id(0)
    kk = pl.program_id(1)
    tm = of_ref.shape[0]
    x_copy = pltpu.make_async_copy(x_hbm.at[pl.ds(pl.multiple_of(i * tm, tm), tm), :], x_buf, x_sem)

    @pl.when(kk == 0)
    def _():
        x_copy.start()
        of_ref[...] = jnp.dot(h_ref[...], w_ref[...], preferred_element_type=F32)

    @pl.when(kk > 0)
    def _():
        of_ref[...] += jnp.dot(h_ref[...], w_ref[...], preferred_element_type=F32)

    @pl.when(kk == n_k - 1)
    def _():
        x_copy.wait()
        for r0 in range(0, tm, LN_ROWS):
            rows = slice(r0, r0 + LN_ROWS)
            y = of_ref[rows, :] + ALPHA * x_buf[rows, :]
            mu = jnp.mean(y, axis=-1, keepdims=True)
            d = y - mu
            var = jnp.mean(d * d, axis=-1, keepdims=True)
            r = d * lax.rsqrt(var + LN_EPS) * g_ref[...] + b_ref[...]
            of_ref[rows, :] = r
            ob_ref[rows, :] = r.astype(BF16)


def _ln_contraction_tile(k):
    cap = (6 * 1024 * 1024) // (2 * D_MODEL)
    return max(t for t in range(LANES, cap + 1, LANES) if k % t == 0)


def _dense_ln(h, w_bf16, layer, x_res, gain, bias, name):
    m, k = h.shape
    d = w_bf16.shape[2]
    tk = _ln_contraction_tile(k)
    assert m % TM_LN == 0 and k % tk == 0
    n_k = k // tk
    return pl.pallas_call(
        functools.partial(_dense_ln_body, n_k=n_k),
        grid=(m // TM_LN, n_k),
        in_specs=[pl.BlockSpec((TM_LN, tk), lambda i, kk: (i, kk)),
                  pl.BlockSpec((None, tk, d), lambda i, kk: (layer, kk, 0)),
                  pl.BlockSpec(memory_space=pl.ANY),
                  pl.BlockSpec((1, d), lambda i, kk: (0, 0)),
                  pl.BlockSpec((1, d), lambda i, kk: (0, 0))],
        out_specs=[pl.BlockSpec((TM_LN, d), lambda i, kk: (i, 0)),
                   pl.BlockSpec((TM_LN, d), lambda i, kk: (i, 0))],
        out_shape=[jax.ShapeDtypeStruct((m, d), F32), jax.ShapeDtypeStruct((m, d), BF16)],
        scratch_shapes=[pltpu.VMEM((TM_LN, d), F32), pltpu.SemaphoreType.DMA(())],
        compiler_params=_params("arbitrary", "arbitrary"), name=name,
    )(h, w_bf16, x_res, gain.reshape(1, d), bias.reshape(1, d))


def _group_norm_gate(o, gain, gate):
    mu = jnp.mean(o, axis=-1, keepdims=True)
    d = o - mu
    var = jnp.mean(d * d, axis=-1, keepdims=True)
    return gate * (d * lax.rsqrt(var + LN_EPS) * gain)


def _ret_prompt_body(q_ref, k_ref, v_ref, g_ref, gain_ref, dmat_ref, qdec_ref, kdec_ref, cdec_ref,
                     st_prev_ref, o_ref, s_ref):
    del st_prev_ref

    @pl.when(pl.program_id(2) == 0)
    def _():
        s_ref[...] = jnp.zeros_like(s_ref)

    heads = range(RET_HB)
    ks = [slice(h * RET_DK, (h + 1) * RET_DK) for h in heads]
    vs = [slice(h * RET_DV, (h + 1) * RET_DV) for h in heads]
    q = q_ref[...]
    k = k_ref[...]
    qb = q.astype(BF16)
    kb = k.astype(BF16)
    qd = (q * qdec_ref[...]).astype(BF16)
    kd = (k * kdec_ref[...]).astype(BF16)
    a = [(lax.dot_general(qb[:, ks[h]], kb[:, ks[h]], (((1,), (1,)), ((), ())),
                          preferred_element_type=F32) * dmat_ref[h]).astype(BF16) for h in heads]
    s_old = [s_ref[h] for h in heads]
    o = [jnp.dot(a[h], v_ref[:, vs[h]], preferred_element_type=F32)
         + jnp.dot(qd[:, ks[h]], s_old[h].astype(BF16), preferred_element_type=F32) for h in heads]
    for h in heads:
        s_ref[h] = cdec_ref[:, vs[h]] * s_old[h] + lax.dot_general(
            kd[:, ks[h]], v_ref[:, vs[h]], (((0,), (0,)), ((), ())), preferred_element_type=F32)
    o_ref[...] = jnp.concatenate(
        [_group_norm_gate(o[h], gain_ref[:, vs[h]], g_ref[:, vs[h]]) for h in heads], axis=1).astype(BF16)


def _ret_decay_tables(c):
    lg = jnp.log(1.0 - 2.0 ** (-5.0 - jnp.arange(RET_HEADS, dtype=F32)))
    idx = jnp.arange(c, dtype=F32)
    rel = idx[:, None] - idx[None, :]
    causal = rel >= 0
    dmat = jnp.where(causal[None], jnp.exp(jnp.where(causal, rel, 0.0)[None] * lg[:, None, None]), 0.0)
    qdec = jnp.exp((idx + 1)[None, :] * lg[:, None])
    kdec = jnp.exp((c - 1 - idx)[None, :] * lg[:, None])
    cdec = jnp.exp(c * lg)
    qdec = jnp.broadcast_to(qdec[:, :, None], (RET_HEADS, c, RET_DK))
    kdec = jnp.broadcast_to(kdec[:, :, None], (RET_HEADS, c, RET_DK))
    cdec = jnp.broadcast_to(cdec[:, None, None], (RET_HEADS, 1, RET_DV))
    return dmat, qdec, kdec, cdec


def _ret_prompt(q, k, v, g, gain, layer_j, st_prev):
    c = RET_C
    nc = SEQ // c
    dmat, qdec, kdec, cdec = _ret_decay_tables(c)
    qdec = jnp.transpose(qdec, (1, 0, 2)).reshape(c, RET_KD)
    kdec = jnp.transpose(kdec, (1, 0, 2)).reshape(c, RET_KD)
    cdec = jnp.transpose(cdec, (1, 0, 2)).reshape(1, RET_VD)
    kw = RET_HB * RET_DK
    vw = RET_HB * RET_DV
    row = lambda b, h, ci: (b * nc + ci, h)
    lane = lambda b, h, ci: (0, h)
    in_specs = [pl.BlockSpec((c, kw), row), pl.BlockSpec((c, kw), row),
                pl.BlockSpec((c, vw), row), pl.BlockSpec((c, vw), row),
                pl.BlockSpec((1, vw), lane),
                pl.BlockSpec((RET_HB, c, c), lambda b, h, ci: (h, 0, 0)), pl.BlockSpec((c, kw), lane),
                pl.BlockSpec((c, kw), lane), pl.BlockSpec((1, vw), lane)]
    operands = [q, k, v, g, gain.reshape(1, RET_VD), dmat, qdec, kdec, cdec]
    body, aliases = _with_optional_alias(_ret_prompt_body, in_specs, operands, st_prev, out_index=1)
    return pl.pallas_call(
        body,
        grid=(BATCH, RET_HEADS // RET_HB, nc),
        in_specs=in_specs,
        out_specs=[pl.BlockSpec((c, vw), row),
                   pl.BlockSpec((None, None, RET_HB, RET_DK, RET_DV),
                                lambda b, h, ci: (layer_j, b, h, 0, 0))],
        out_shape=[jax.ShapeDtypeStruct((M_ROWS, RET_VD), BF16),
                   jax.ShapeDtypeStruct((N_RET, BATCH, RET_HEADS, RET_DK, RET_DV), F32)],
        input_output_aliases=aliases,
        compiler_params=_params("parallel", "parallel", "arbitrary"), name="ret_prompt",
    )(*operands)


def _ret_sample_body(q_ref, k_ref, v_ref, g_ref, gain_ref, dmat_ref, qdec_ref, kdec_ref, cdec_ref,
                     s0_ref, o_prev_ref, st_prev_ref, o_ref, s_ref):
    del o_prev_ref, st_prev_ref
    units = [(bi, h) for bi in range(RET_S_BB) for h in range(RET_S_HB)]
    rows = lambda bi: slice(bi * DEC_SEQ, (bi + 1) * DEC_SEQ)
    ks = lambda h: slice(h * RET_DK, (h + 1) * RET_DK)
    vs = lambda h: slice(h * RET_DV, (h + 1) * RET_DV)
    v_all = v_ref[...].astype(F32)
    q = [q_ref[rows(bi), ks(h)] for bi, h in units]
    k = [k_ref[rows(bi), ks(h)] for bi, h in units]
    v = [v_all[rows(bi), vs(h)].astype(BF16) for bi, h in units]
    a = [(lax.dot_general(q[u].astype(BF16), k[u].astype(BF16), (((1,), (1,)), ((), ())),
                          preferred_element_type=F32) * dmat_ref[h]).astype(BF16)
         for u, (bi, h) in enumerate(units)]
    o = [jnp.dot(a[u], v[u], preferred_element_type=F32)
         + jnp.dot((q[u] * qdec_ref[h]).astype(BF16), s0_ref[0, bi, h].astype(BF16),
                   preferred_element_type=F32) for u, (bi, h) in enumerate(units)]
    for u, (bi, h) in enumerate(units):
        s_ref[0, bi, h] = cdec_ref[h] * s0_ref[0, bi, h] + lax.dot_general(
            (k[u] * kdec_ref[h]).astype(BF16), v[u], (((0,), (0,)), ((), ())), preferred_element_type=F32)
    gated = [_group_norm_gate(o[u], gain_ref[:, vs(h)], g_ref[rows(bi), vs(h)])
             for u, (bi, h) in enumerate(units)]
    o_ref[...] = jnp.concatenate(
        [jnp.concatenate(gated[bi * RET_S_HB:(bi + 1) * RET_S_HB], axis=1) for bi in range(RET_S_BB)],
        axis=0).astype(BF16)


def _ret_sample(q, k, v, g, gain, state_all, layer_j, o_prev, st_prev):
    c = DEC_SEQ
    dmat, qdec, kdec, cdec = _ret_decay_tables(c)
    rows_per = RET_S_BB * DEC_SEQ
    row0 = MP_ROWS // rows_per
    n_hb = RET_HEADS // RET_S_HB
    row = lambda bb, hb: (row0 + bb, hb)
    hsel = lambda bb, hb: (hb, 0, 0)
    st_block = (1, RET_S_BB, RET_S_HB, RET_DK, RET_DV)
    st_map = lambda bb, hb: (layer_j, bb, hb, 0, 0)
    in_specs = [pl.BlockSpec((rows_per, RET_S_HB * RET_DK), row),
                pl.BlockSpec((rows_per, RET_S_HB * RET_DK), row),
                pl.BlockSpec((rows_per, RET_S_HB * RET_DV), row),
                pl.BlockSpec((rows_per, RET_S_HB * RET_DV), row),
                pl.BlockSpec((1, RET_S_HB * RET_DV), lambda bb, hb: (0, hb)),
                pl.BlockSpec((RET_S_HB, c, c), hsel),
                pl.BlockSpec((RET_S_HB, c, RET_DK), hsel),
                pl.BlockSpec((RET_S_HB, c, RET_DK), hsel),
                pl.BlockSpec((RET_S_HB, 1, RET_DV), hsel),
                pl.BlockSpec(st_block, st_map),
                pl.BlockSpec(memory_space=pl.ANY)]
    operands = [q, k, v, g, gain.reshape(1, RET_VD), dmat, qdec, kdec, cdec, state_all, o_prev]
    aliases = {10: 0}
    if st_prev is not None:
        in_specs.append(pl.BlockSpec(memory_space=pl.ANY))
        operands.append(st_prev)
        aliases[11] = 1
        body = _ret_sample_body
    else:
        body = lambda *refs: _ret_sample_body(*refs[:11], None, *refs[11:])
    return pl.pallas_call(
        body,
        grid=(DEC_BATCH // RET_S_BB, n_hb),
        in_specs=in_specs,
        out_specs=[pl.BlockSpec((rows_per, RET_S_HB * RET_DV), row), pl.BlockSpec(st_block, st_map)],
        out_shape=[jax.ShapeDtypeStruct((M_ROWS, RET_VD), BF16),
                   jax.ShapeDtypeStruct((N_RET, DEC_BATCH, RET_HEADS, RET_DK, RET_DV), F32)],
        input_output_aliases=aliases,
        compiler_params=_params("parallel", "parallel"), name="ret_sample",
    )(*operands)


def _rms_norm_gate(o, gain, gate):
    return o * lax.rsqrt(jnp.mean(o * o, axis=-1, keepdims=True) + LN_EPS) * gain * gate


def _diag_tiles(g2_blk, q_blk, g2_row, k_row):
    n = g2_blk.shape[0]
    return [jnp.exp2(g2_blk - g2_row(j)) * q_blk * k_row(j) for j in range(n)]


def _select_diag(sums, n, lane_base):
    lanes = sums.shape[1]
    rel = lax.broadcasted_iota(jnp.int32, (n, lanes), 1) - lane_base
    row = lax.broadcasted_iota(jnp.int32, (n, lanes), 0)
    acc = jnp.zeros((n, lanes), F32)
    for j in range(n):
        acc = jnp.where(rel == j, sums[j * n:(j + 1) * n, :], acc)
    return jnp.where(row >= rel, acc, 0.0)


def _gla_prompt_body(q_ref, k_ref, lf_ref, v_ref, g_ref, gain_ref, ltri_ref, st_prev_ref,
                     o_ref, st_ref, s_scr, g_scr):
    del st_prev_ref
    ci = pl.program_id(2)
    c = GLA_C
    n = GLA_SUB

    @pl.when(ci == 0)
    def _():
        s_scr[...] = jnp.zeros_like(s_scr)

    width = q_ref.shape[1]
    heads = [slice(hh * HG_DK, (hh + 1) * HG_DK) for hh in range(GLA_HB)]
    ones_b = jnp.ones((HG_DK, c), BF16)
    row = lax.broadcasted_iota(jnp.int32, (c, width), 0)
    ri = lax.broadcasted_iota(jnp.int32, (c, c), 0)
    cj = lax.broadcasted_iota(jnp.int32, (c, c), 1)

    g2 = LOG2_E * jnp.dot(ltri_ref[...], lf_ref[...], precision=lax.Precision.HIGHEST,
                          preferred_element_type=F32)
    g_scr[...] = g2
    q = q_ref[...]
    k = k_ref[...]

    a = [jnp.zeros((c, c), F32) for _ in heads]
    s = c // 2
    while s >= n:
        ref_rows = [jnp.broadcast_to(g_scr[m + s - 1:m + s, :], (2 * s, width)) for m in range(0, c, 2 * s)]
        ref = ref_rows[0] if len(ref_rows) == 1 else jnp.concatenate(ref_rows, axis=0)
        e = jnp.exp2(-jnp.abs(g2 - ref))
        second = (row & s) != 0
        qt = jnp.where(second, q * e, 0.0).astype(BF16)
        kt = jnp.where(second, 0.0, k * e).astype(BF16)
        for hh, sl in enumerate(heads):
            p = lax.dot_general(qt[:, sl], kt[:, sl], (((1,), (1,)), ((), ())), preferred_element_type=F32)
            if 2 * s < c:
                sh = int(math.log2(2 * s))
                p = jnp.where((ri >> sh) == (cj >> sh), p, 0.0)
            a[hh] = a[hh] + p
        s //= 2

    tiles = []
    for sl in heads:
        for m in range(0, c, n):
            tiles += _diag_tiles(g2[m:m + n, sl], q[m:m + n, sl],
                                 lambda j, m=m, sl=sl: g_scr[m + j:m + j + 1, sl],
                                 lambda j, m=m, sl=sl: k_ref[m + j:m + j + 1, sl])
    sums = jnp.dot(jnp.concatenate(tiles, axis=0).astype(BF16), ones_b, preferred_element_type=F32)

    qe = (q * jnp.exp2(g2)).astype(BF16)
    g2_last = g_scr[c - 1:c, :]
    kd = (k * jnp.exp2(g2_last - g2)).astype(BF16)
    decay = jnp.exp2(g2_last)
    outs = []
    for hh, sl in enumerate(heads):
        base = hh * c * n
        diag = [_select_diag(sums[base + m * n:base + (m + n) * n, :], n, m) for m in range(0, c, n)]
        a_h = (a[hh] + jnp.concatenate(diag, axis=0)).astype(BF16)
        v = v_ref[:, sl]
        s_t = s_scr[hh]
        o = jnp.dot(a_h, v, preferred_element_type=F32)
        o = o + lax.dot_general(qe[:, sl], s_t.astype(BF16), (((1,), (1,)), ((), ())),
                                preferred_element_type=F32)
        s_scr[hh] = decay[:, sl] * s_t + lax.dot_general(
            v, kd[:, sl], (((0,), (0,)), ((), ())), preferred_element_type=F32)
        outs.append(_rms_norm_gate(o, gain_ref[...], g_ref[:, sl]))

    o_ref[...] = jnp.concatenate(outs, axis=1).astype(BF16)

    @pl.when(ci == pl.num_programs(2) - 1)
    def _():
        for hh in range(GLA_HB):
            st_ref[hh] = s_scr[hh].T


def _gla_prompt(q, k, lf, v, g, gain, layer_j, st_prev):
    c = GLA_C
    nc = SEQ // c
    w = GLA_HB * HG_DK
    ltri = (jnp.arange(c)[:, None] >= jnp.arange(c)[None, :]).astype(F32)
    row = lambda b, hb, ci: (b * nc + ci, hb)
    in_specs = [pl.BlockSpec((c, w), row), pl.BlockSpec((c, w), row), pl.BlockSpec((c, w), row),
                pl.BlockSpec((c, w), row), pl.BlockSpec((c, w), row),
                pl.BlockSpec((1, HG_DV), lambda b, hb, ci: (0, 0)),
                pl.BlockSpec((c, c), lambda b, hb, ci: (0, 0))]
    operands = [q, k, lf, v, g, gain.reshape(1, HG_DV), ltri]
    body, aliases = _with_optional_alias(_gla_prompt_body, in_specs, operands, st_prev, out_index=1)
    return pl.pallas_call(
        body,
        grid=(BATCH, HG_HEADS // GLA_HB, nc),
        in_specs=in_specs,
        out_specs=[pl.BlockSpec((c, w), row),
                   pl.BlockSpec((None, None, GLA_HB, HG_DK, HG_DV),
                                lambda b, hb, ci: (layer_j, b, hb, 0, 0))],
        out_shape=[jax.ShapeDtypeStruct((M_ROWS, D_MODEL), BF16),
                   jax.ShapeDtypeStruct((N_HGRN, BATCH, HG_HEADS, HG_DK, HG_DV), F32)],
        input_output_aliases=aliases,
        scratch_shapes=[pltpu.VMEM((GLA_HB, HG_DV, HG_DK), F32), pltpu.VMEM((c, w), F32)],
        compiler_params=_params("parallel", "parallel", "arbitrary"), name="gla_prompt",
    )(*operands)


def _gla_sample_body(q_ref, k_ref, lf_ref, v_ref, g_ref, gain_ref, s0_ref, o_prev_ref, st_prev_ref,
                     o_ref, s_ref, g_scr):
    del o_prev_ref, st_prev_ref
    n = DEC_SEQ
    rows_per = SAMPLE_BB * n
    width = q_ref.shape[1]
    rix = lax.broadcasted_iota(jnp.int32, (rows_per, width), 0) & (n - 1)
    g2_all = lf_ref[...]
    sh = 1
    while sh < n:
        g2_all = g2_all + jnp.where(rix >= sh, pltpu.roll(g2_all, sh, axis=0), 0.0)
        sh *= 2
    g2_all = LOG2_E * g2_all
    g_scr[...] = g2_all
    q_all = q_ref[...]
    g2_last = jnp.concatenate([jnp.broadcast_to(g_scr[(bi + 1) * n - 1:(bi + 1) * n, :], (n, width))
                               for bi in range(SAMPLE_BB)], axis=0)
    qe_all = q_all * jnp.exp2(g2_all)
    kd_all = k_ref[...] * jnp.exp2(g2_last - g2_all)
    decay_all = jnp.exp2(g2_last)
    v_all = v_ref[...].astype(F32)
    ones_b = jnp.ones((HG_DK, HG_DK), BF16)
    zpad = jnp.zeros((HG_DK - n, HG_DV), F32)
    units = [(bi, h) for bi in range(SAMPLE_BB) for h in range(HG_HEADS)]
    span = lambda bi, h: (slice(bi * n, (bi + 1) * n), slice(h * HG_DK, (h + 1) * HG_DK))

    tiles = []
    for bi, h in units:
        rows, sl = span(bi, h)
        tiles += _diag_tiles(g2_all[rows, sl], q_all[rows, sl],
                             lambda j, bi=bi, sl=sl: g_scr[bi * n + j:bi * n + j + 1, sl],
                             lambda j, bi=bi, sl=sl: k_ref[bi * n + j:bi * n + j + 1, sl])
    sums = jnp.dot(jnp.concatenate(tiles, axis=0).astype(BF16), ones_b, preferred_element_type=F32)

    v_pads, o_units = [], []
    for u, (bi, h) in enumerate(units):
        rows, sl = span(bi, h)
        a = _select_diag(sums[u * n * n:(u + 1) * n * n, :], n, 0)
        v_pad = jnp.concatenate([v_all[rows, sl], zpad], axis=0).astype(BF16)
        v_pads.append(v_pad)
        o = jnp.dot(a.astype(BF16), v_pad, preferred_element_type=F32)
        o_units.append(o + jnp.dot(qe_all[rows, sl].astype(BF16), s0_ref[0, bi, h].astype(BF16),
                                   preferred_element_type=F32))
    for u, (bi, h) in enumerate(units):
        rows, sl = span(bi, h)
        kd_pad = jnp.concatenate([kd_all[rows, sl], zpad], axis=0).astype(BF16)
        decay_col = jnp.broadcast_to(decay_all[bi * n:bi * n + 1, sl], (HG_DV, HG_DK)).T
        s_ref[0, bi, h] = decay_col * s0_ref[0, bi, h] + lax.dot_general(
            kd_pad, v_pads[u], (((0,), (0,)), ((), ())), preferred_element_type=F32)
    outs = []
    for bi in range(SAMPLE_BB):
        per_head = [_rms_norm_gate(o_units[bi * HG_HEADS + h], gain_ref[...], g_ref[span(bi, h)])
                    for h in range(HG_HEADS)]
        outs.append(jnp.concatenate(per_head, axis=1))
    o_ref[...] = jnp.concatenate(outs, axis=0).astype(BF16)


def _gla_sample(q, k, lf, v, g, gain, state_all, layer_j, o_prev, st_prev):
    rows_per = SAMPLE_BB * DEC_SEQ
    row0 = MP_ROWS // rows_per
    row = lambda bb: (row0 + bb, 0)
    st_block = (1, SAMPLE_BB, HG_HEADS, HG_DK, HG_DV)
    st_map = lambda bb: (layer_j, bb, 0, 0, 0)
    in_specs = [pl.BlockSpec((rows_per, D_MODEL), row)] * 5 + [
        pl.BlockSpec((1, HG_DV), lambda bb: (0, 0)),
        pl.BlockSpec(st_block, st_map),
        pl.BlockSpec(memory_space=pl.ANY)]
    operands = [q, k, lf, v, g, gain.reshape(1, HG_DV), state_all, o_prev]
    aliases = {7: 0}
    if st_prev is not None:
        in_specs.append(pl.BlockSpec(memory_space=pl.ANY))
        operands.append(st_prev)
        aliases[8] = 1
        body = _gla_sample_body
    else:
        body = lambda *refs: _gla_sample_body(*refs[:8], None, *refs[8:])
    return pl.pallas_call(
        body,
        grid=(DEC_BATCH // SAMPLE_BB,),
        in_specs=in_specs,
        out_specs=[pl.BlockSpec((rows_per, D_MODEL), row), pl.BlockSpec(st_block, st_map)],
        out_shape=[jax.ShapeDtypeStruct((M_ROWS, D_MODEL), BF16),
                   jax.ShapeDtypeStruct((N_HGRN, DEC_BATCH, HG_HEADS, HG_DK, HG_DV), F32)],
        input_output_aliases=aliases,
        scratch_shapes=[pltpu.VMEM((rows_per, D_MODEL), F32)],
        compiler_params=_params("parallel"), name="gla_sample",
    )(*operands)


def _stack_rows_body(xp_ref, xs_ref, of_ref, ob_ref, *, n_prompt_blocks):
    i = pl.program_id(0)

    @pl.when(i < n_prompt_blocks)
    def _():
        of_ref[...] = xp_ref[...]
        ob_ref[...] = xp_ref[...].astype(BF16)

    @pl.when(i >= n_prompt_blocks)
    def _():
        of_ref[...] = xs_ref[...]
        ob_ref[...] = xs_ref[...].astype(BF16)


def _stack_rows(xp, xs):
    rows = STACK_ROWS
    n_p, n_s = xp.shape[0] // rows, xs.shape[0] // rows
    assert xp.shape[0] % rows == 0 and xs.shape[0] % rows == 0
    d = xp.shape[1]
    return pl.pallas_call(
        functools.partial(_stack_rows_body, n_prompt_blocks=n_p),
        grid=(n_p + n_s,),
        in_specs=[pl.BlockSpec((rows, d), lambda i: (jnp.minimum(i, n_p - 1), 0)),
                  pl.BlockSpec((rows, d), lambda i: (jnp.maximum(i - n_p, 0), 0))],
        out_specs=[pl.BlockSpec((rows, d), lambda i: (i, 0)), pl.BlockSpec((rows, d), lambda i: (i, 0))],
        out_shape=[jax.ShapeDtypeStruct((n_p * rows + n_s * rows, d), F32),
                   jax.ShapeDtypeStruct((n_p * rows + n_s * rows, d), BF16)],
        compiler_params=_params("arbitrary"), name="stack_rows",
    )(xp, xs)


def _rope_tables():
    inv = jnp.repeat(1.0 / (ROPE_BASE ** jnp.linspace(0.0, 1.0, RET_DK // 2, dtype=F32)), 2)
    pos = jnp.concatenate([0 + jnp.arange(SEQ, dtype=F32),
                           jnp.tile(PAST_LEN + jnp.arange(DEC_SEQ, dtype=F32), DEC_BATCH)])
    ang = pos[:, None] * inv[None, :]
    sign = jnp.tile(jnp.array([-1.0, 1.0], F32), RET_DK // 2)
    return jnp.cos(ang), jnp.sin(ang) * sign[None, :]


def _rope_block(j, i):
    per_seq = SEQ // TM
    return (jnp.where(i < MP_ROWS // TM, i % per_seq, per_seq + i - MP_ROWS // TM), 0)


def _retention_layer(xb, w_in, gn_gain, state_all, layer_j, st_prev, rope):
    cos, sin = rope
    rope_specs = [pl.BlockSpec((TM, RET_DK), _rope_block)] * 2
    (q,) = _dense(xb, w_in, layer_j, [0], RET_KD, functools.partial(_ep_rotary, scale=1.0), [F32],
                  extras=(cos, sin), extra_specs=rope_specs, tn=TN_WIDE, name="ret_q")
    (k,) = _dense(xb, w_in, layer_j, [RET_KD], RET_KD,
                  functools.partial(_ep_rotary, scale=RET_DK ** -0.5), [F32],
                  extras=(cos, sin), extra_specs=rope_specs, tn=TN_WIDE, name="ret_k")
    (v,) = _dense(xb, w_in, layer_j, [2 * RET_KD], RET_VD, _ep_plain, [BF16], tn=TN_WIDE, name="ret_v")
    (g,) = _dense(xb, w_in, layer_j, [2 * RET_KD + RET_VD], RET_VD, _ep_silu, [F32], tn=TN_WIDE,
                  name="ret_g")
    o, st_p = _ret_prompt(q, k, v, g, gn_gain, layer_j, st_prev[0])
    o, st_s = _ret_sample(q, k, v, g, gn_gain, state_all, layer_j, o, st_prev[1])
    return o, (st_p, st_s)


def _hgrn_layer(xb, w_in, lb_logits, norm_gain, state_all, layer_j, st_prev):
    d = D_MODEL
    logit_spec = [pl.BlockSpec((N_HGRN, TN_WIDE), lambda j, i: (0, j))]
    (q,) = _dense(xb, w_in, layer_j, [0], d, functools.partial(_ep_silu_scaled, scale=HG_DK ** -0.5),
                  [F32], tn=TN_WIDE, name="hg_q")
    lf, k = _dense(xb, w_in, layer_j, [d], d, functools.partial(_ep_forget, layer=layer_j), [F32, F32],
                   extras=(lb_logits,), extra_specs=logit_spec, tn=TN_WIDE, name="hg_f")
    (v,) = _dense(xb, w_in, layer_j, [2 * d], d, _ep_plain, [BF16], tn=TN_WIDE, name="hg_v")
    (g,) = _dense(xb, w_in, layer_j, [3 * d], d, _ep_silu, [F32], tn=TN_WIDE, name="hg_g")
    o, st_p = _gla_prompt(q, k, lf, v, g, norm_gain, layer_j, st_prev[0])
    o, st_s = _gla_sample(q, k, lf, v, g, norm_gain, state_all, layer_j, o, st_prev[1])
    return o, (st_p, st_s)


def kernel(x_prompt, x_sample, state_ret, state_hgrn, ret_w_in, ret_gn_gain, ret_w_out,
           hgrn_w_in, hgrn_lb_logits, hgrn_norm_gain, hgrn_w_out,
           ln_mix_g, ln_mix_b, ffn_w_in, ffn_w_out, ln_ffn_g, ln_ffn_b):
    assert x_prompt.shape == (BATCH, SEQ, D_MODEL) and x_sample.shape == (DEC_BATCH, DEC_SEQ, D_MODEL)
    assert state_ret.shape == (N_RET, DEC_BATCH, RET_HEADS, RET_DK, RET_DV)
    assert state_hgrn.shape == (N_HGRN, DEC_BATCH, HG_HEADS, HG_DK, HG_DV)
    assert ffn_w_in.shape == (DEPTH, D_MODEL, 2 * D_FF)

    x, xb = _stack_rows(x_prompt.reshape(MP_ROWS, D_MODEL), x_sample.reshape(MS_ROWS, D_MODEL))
    rope = _rope_tables()
    ret_w_out_b = ret_w_out.astype(BF16)
    hgrn_w_out_b = hgrn_w_out.astype(BF16)
    ffn_w_out_b = ffn_w_out.astype(BF16)

    ret_st = (None, None)
    hg_st = (None, None)
    for layer in range(DEPTH):
        j = layer // 2
        if layer % 2 == 0:
            o, ret_st = _retention_layer(xb, ret_w_in, ret_gn_gain[j], state_ret, j, ret_st, rope)
            w_out = ret_w_out_b
        else:
            o, hg_st = _hgrn_layer(xb, hgrn_w_in, hgrn_lb_logits, hgrn_norm_gain[j], state_hgrn, j, hg_st)
            w_out = hgrn_w_out_b
        x, xb = _dense_ln(o, w_out, j, x, ln_mix_g[layer], ln_mix_b[layer], name="mix_out")
        (h,) = _dense(xb, ffn_w_in, layer, [0, D_FF], D_FF, _ep_swiglu, [BF16], tm=TM_TALL,
                      hoist_cast=False, name="ffn_in")
        x, xb = _dense_ln(h, ffn_w_out_b, layer, x, ln_ffn_g[layer], ln_ffn_b[layer], name="ffn_out")

    return (x[:MP_ROWS].reshape(BATCH, SEQ, D_MODEL),
            x[MP_ROWS:].reshape(DEC_BATCH, DEC_SEQ, D_MODEL),
            ret_st[0], ret_st[1], hg_st[0], hg_st[1])
```

```python
import functools
import math

import jax
import jax.numpy as jnp
from jax import lax
from jax.experimental import pallas as pl
from jax.experimental.pallas import tpu as pltpu

F32 = jnp.float32
BF16 = jnp.bfloat16

D_MODEL = 2048
BATCH = 4
SEQ = 2048
DEPTH = 4
DEC_BATCH = 128
DEC_SEQ = 8
PAST_LEN = 16384
N_RET = 2
N_HGRN = 2
RET_HEADS = 8
RET_DK = 256
RET_DV = 512
RET_KD = RET_HEADS * RET_DK
RET_VD = RET_HEADS * RET_DV
HG_HEADS = 16
HG_DK = 128
HG_DV = 128
D_FF = 5632
ROPE_BASE = 10000.0
ALPHA = (2 * DEPTH) ** 0.25
LN_EPS = 1e-5
LB_FLOOR = 1e-30
LOG2_E = 1.4426950408889634

MP_ROWS = BATCH * SEQ
MS_ROWS = DEC_BATCH * DEC_SEQ
M_ROWS = MP_ROWS + MS_ROWS

VMEM_LIMIT_BYTES = 56 * 1024 * 1024

LANES = 128

TM = 1024
TM_TALL = 1536
TN_WIDE = 1024
TN = 512
TM_LN = 1024
TM_LN_SHORT = 512
LN_ROWS = 256
EPI_COLS = 256
STACK_ROWS = 512

RET_C = 256
RET_HB = 4
GLA_C = 128
GLA_SUB = 8
GLA_HB = 8
SAMPLE_BB = 2
RET_S_BB = 4
RET_S_HB = 4


def _params(*sem):
    return pltpu.CompilerParams(dimension_semantics=sem, vmem_limit_bytes=VMEM_LIMIT_BYTES)


def _with_optional_alias(body, in_specs, operands, prev, out_index):
    n_in = len(operands)
    if prev is None:
        return (lambda *refs: body(*refs[:n_in], None, *refs[n_in:])), {}
    in_specs.append(pl.BlockSpec(memory_space=pl.ANY))
    operands.append(prev)
    return body, {n_in: out_index}


def _sigmoid(x):
    return 1.0 / (1.0 + jnp.exp(-x))


def _silu(x):
    return x * _sigmoid(x)


def _dense_body(*refs, n_w, n_extra, n_out, epilogue):
    x_ref = refs[0]
    w_refs = refs[1:1 + n_w]
    extra_refs = refs[1 + n_w:1 + n_w + n_extra]
    out_refs = refs[1 + n_w + n_extra:1 + n_w + n_extra + n_out]
    wb_refs = refs[1 + n_w + n_extra + n_out:]

    if wb_refs:
        @pl.when(pl.program_id(1) == 0)
        def _():
            for w_ref, wb_ref in zip(w_refs, wb_refs):
                wb_ref[...] = w_ref[...].astype(BF16)
        rhs = [lambda cols, r=wb_ref: r[:, cols] for wb_ref in wb_refs]
    else:
        rhs = [lambda cols, r=w_ref: r[:, cols].astype(BF16) for w_ref in w_refs]

    x = x_ref[...]
    for c0 in range(0, out_refs[0].shape[1], EPI_COLS):
        cols = slice(c0, c0 + EPI_COLS)
        accs = [jnp.dot(x, get(cols), preferred_element_type=F32) for get in rhs]
        epilogue(accs, extra_refs, out_refs, cols)


def _dense(x, w, layer, col_offs, n_cols, epilogue, out_dtypes, extras=(), extra_specs=(), tn=TN, tm=TM,
           hoist_cast=True, name="dense"):
    m, k = x.shape
    assert m % tm == 0 and n_cols % tn == 0 and all(c % tn == 0 for c in col_offs)
    grid = (n_cols // tn, m // tm)
    in_specs = [pl.BlockSpec((tm, k), lambda j, i: (i, 0))]
    for off in col_offs:
        in_specs.append(pl.BlockSpec((None, k, tn),
                                     functools.partial(lambda j, i, o: (layer, 0, j + o), o=off // tn)))
    in_specs += list(extra_specs)
    out_specs = [pl.BlockSpec((tm, tn), lambda j, i: (i, j)) for _ in out_dtypes]
    out_shape = [jax.ShapeDtypeStruct((m, n_cols), dt) for dt in out_dtypes]
    body = functools.partial(_dense_body, n_w=len(col_offs), n_extra=len(extras),
                             n_out=len(out_dtypes), epilogue=epilogue)
    return pl.pallas_call(
        body, grid=grid, in_specs=in_specs, out_specs=out_specs, out_shape=out_shape,
        scratch_shapes=[pltpu.VMEM((k, tn), BF16) for _ in col_offs] if hoist_cast else [],
        compiler_params=_params("arbitrary", "arbitrary"), name=name,
    )(x, *([w] * len(col_offs)), *extras)


def _ep_plain(accs, extra_refs, out_refs, cols):
    out_refs[0][:, cols] = accs[0].astype(out_refs[0].dtype)


def _ep_silu(accs, extra_refs, out_refs, cols):
    out_refs[0][:, cols] = _silu(accs[0]).astype(out_refs[0].dtype)


def _ep_silu_scaled(accs, extra_refs, out_refs, cols, *, scale):
    out_refs[0][:, cols] = (_silu(accs[0]) * scale).astype(out_refs[0].dtype)


def _ep_swiglu(accs, extra_refs, out_refs, cols):
    out_refs[0][:, cols] = (_silu(accs[0]) * accs[1]).astype(out_refs[0].dtype)


def _ep_rotary(accs, extra_refs, out_refs, cols, *, scale):
    cos_ref, sin_ref = extra_refs
    t = accs[0]
    assert t.shape[1] == RET_DK
    lane = lax.broadcasted_iota(jnp.int32, t.shape, 1)
    swapped = jnp.where((lane & 1) == 0,
                        pltpu.roll(t, t.shape[1] - 1, axis=1),
                        pltpu.roll(t, 1, axis=1))
    r = t * cos_ref[...] + swapped * sin_ref[...]
    if scale != 1.0:
        r = r * scale
    out_refs[0][:, cols] = r.astype(out_refs[0].dtype)


def _ep_forget(accs, extra_refs, out_refs, cols, *, layer):
    logits_ref = extra_refs[0]
    rows = [logits_ref[r:r + 1, cols] for r in range(N_HGRN)]
    mx = functools.reduce(jnp.maximum, rows)
    es = [jnp.exp(r - mx) for r in rows]
    den = functools.reduce(lambda a, b: a + b, es)
    ps = [e / den for e in es]
    lb = functools.reduce(lambda a, b: a + b, ps[:layer + 1]) - ps[0]
    fpre = accs[0]
    t = jnp.exp(-jnp.abs(fpre))
    r = 1.0 / (1.0 + t)
    tr = t * r
    pos = fpre >= 0.0
    one_m_lb = 1.0 - lb
    out_refs[0][:, cols] = jnp.log(jnp.maximum(lb, LB_FLOOR) + one_m_lb * jnp.where(pos, r, tr))
    out_refs[1][:, cols] = one_m_lb * jnp.where(pos, tr, r)


def _dense_ln_body(h_ref, w_ref, x_hbm, g_ref, b_ref, of_ref, ob_ref, x_buf, x_sem, *, n_k):
    i = pl.program_id(0)
    kk = pl.program_id(1)
    tm = of_ref.shape[0]
    x_copy = pltpu.make_async_copy(x_hbm.at[pl.ds(pl.multiple_of(i * tm, tm), tm), :], x_buf, x_sem)

    @pl.when(kk == 0)
    def _():
        x_copy.start()
        of_ref[...] = jnp.dot(h_ref[...], w_ref[...], preferred_element_type=F32)

    @pl.when(kk > 0)
    def _():
        of_ref[...] += jnp.dot(h_ref[...], w_ref[...], preferred_element_type=F32)

    @pl.when(kk == n_k - 1)
    def _():
        x_copy.wait()
        for r0 in range(0, tm, LN_ROWS):
            rows = slice(r0, r0 + LN_ROWS)
            y = of_ref[rows, :] + ALPHA * x_buf[rows, :]
            mu = jnp.mean(y, axis=-1, keepdims=True)
            d = y - mu
            var = jnp.mean(d * d, axis=-1, keepdims=True)
            r = d * lax.rsqrt(var + LN_EPS) * g_ref[...] + b_ref[...]
            of_ref[rows, :] = r
            ob_ref[rows, :] = r.astype(BF16)


def _ln_tiles(k):
    mib = 1024 * 1024
    if k * D_MODEL * 2 <= 8 * mib:
        return TM_LN_SHORT, k
    cap = (6 * mib) // (2 * D_MODEL)
    return TM_LN, max(t for t in range(LANES, cap + 1, LANES) if k % t == 0)


def _dense_ln(h, w_bf16, layer, x_res, gain, bias, name):
    m, k = h.shape
    d = w_bf16.shape[2]
    tm, tk = _ln_tiles(k)
    assert m % tm == 0 and k % tk == 0
    n_k = k // tk
    return pl.pallas_call(
        functools.partial(_dense_ln_body, n_k=n_k),
        grid=(m // tm, n_k),
        in_specs=[pl.BlockSpec((tm, tk), lambda i, kk: (i, kk)),
                  pl.BlockSpec((None, tk, d), lambda i, kk: (layer, kk, 0)),
                  pl.BlockSpec(memory_space=pl.ANY),
                  pl.BlockSpec((1, d), lambda i, kk: (0, 0)),
                  pl.BlockSpec((1, d), lambda i, kk: (0, 0))],
        out_specs=[pl.BlockSpec((tm, d), lambda i, kk: (i, 0)),
                   pl.BlockSpec((tm, d), lambda i, kk: (i, 0))],
        out_shape=[jax.ShapeDtypeStruct((m, d), F32), jax.ShapeDtypeStruct((m, d), BF16)],
        scratch_shapes=[pltpu.VMEM((tm, d), F32), pltpu.SemaphoreType.DMA(())],
        compiler_params=_params("arbitrary", "arbitrary"), name=name,
    )(h, w_bf16, x_res, gain.reshape(1, d), bias.reshape(1, d))


def _group_norm_gate(o, gain, gate):
    mu = jnp.mean(o, axis=-1, keepdims=True)
    d = o - mu
    var = jnp.mean(d * d, axis=-1, keepdims=True)
    return gate * (d * lax.rsqrt(var + LN_EPS) * gain)


def _ret_prompt_body(q_ref, k_ref, v_ref, g_ref, gain_ref, dmat_ref, qdec_ref, kdec_ref, cdec_ref,
                     st_prev_ref, o_ref, s_ref):
    del st_prev_ref

    @pl.when(pl.program_id(2) == 0)
    def _():
        s_ref[...] = jnp.zeros_like(s_ref)

    heads = range(RET_HB)
    ks = [slice(h * RET_DK, (h + 1) * RET_DK) for h in heads]
    vs = [slice(h * RET_DV, (h + 1) * RET_DV) for h in heads]
    q = q_ref[...]
    k = k_ref[...]
    qb = q.astype(BF16)
    kb = k.astype(BF16)
    qd = (q * qdec_ref[...]).astype(BF16)
    kd = (k * kdec_ref[...]).astype(BF16)
    a = [(lax.dot_general(qb[:, ks[h]], kb[:, ks[h]], (((1,), (1,)), ((), ())),
                          preferred_element_type=F32) * dmat_ref[h]).astype(BF16) for h in heads]
    s_old = [s_ref[h] for h in heads]
    o = [jnp.dot(a[h], v_ref[:, vs[h]], preferred_element_type=F32)
         + jnp.dot(qd[:, ks[h]], s_old[h].astype(BF16), preferred_element_type=F32) for h in heads]
    for h in heads:
        s_ref[h] = cdec_ref[:, vs[h]] * s_old[h] + lax.dot_general(
            kd[:, ks[h]], v_ref[:, vs[h]], (((0,), (0,)), ((), ())), preferred_element_type=F32)
    o_ref[...] = jnp.concatenate(
        [_group_norm_gate(o[h], gain_ref[:, vs[h]], g_ref[:, vs[h]]) for h in heads], axis=1).astype(BF16)


def _ret_decay_tables(c):
    lg = jnp.log(1.0 - 2.0 ** (-5.0 - jnp.arange(RET_HEADS, dtype=F32)))
    idx = jnp.arange(c, dtype=F32)
    rel = idx[:, None] - idx[None, :]
    causal = rel >= 0
    dmat = jnp.where(causal[None], jnp.exp(jnp.where(causal, rel, 0.0)[None] * lg[:, None, None]), 0.0)
    qdec = jnp.exp((idx + 1)[None, :] * lg[:, None])
    kdec = jnp.exp((c - 1 - idx)[None, :] * lg[:, None])
    cdec = jnp.exp(c * lg)
    qdec = jnp.broadcast_to(qdec[:, :, None], (RET_HEADS, c, RET_DK))
    kdec = jnp.broadcast_to(kdec[:, :, None], (RET_HEADS, c, RET_DK))
    cdec = jnp.broadcast_to(cdec[:, None, None], (RET_HEADS, 1, RET_DV))
    return dmat, qdec, kdec, cdec


def _ret_prompt(q, k, v, g, gain, layer_j, st_prev):
    c = RET_C
    nc = SEQ // c
    dmat, qdec, kdec, cdec = _ret_decay_tables(c)
    qdec = jnp.transpose(qdec, (1, 0, 2)).reshape(c, RET_KD)
    kdec = jnp.transpose(kdec, (1, 0, 2)).reshape(c, RET_KD)
    cdec = jnp.transpose(cdec, (1, 0, 2)).reshape(1, RET_VD)
    kw = RET_HB * RET_DK
    vw = RET_HB * RET_DV
    row = lambda b, h, ci: (b * nc + ci, h)
    lane = lambda b, h, ci: (0, h)
    in_specs = [pl.BlockSpec((c, kw), row), pl.BlockSpec((c, kw), row),
                pl.BlockSpec((c, vw), row), pl.BlockSpec((c, vw), row),
                pl.BlockSpec((1, vw), lane),
                pl.BlockSpec((RET_HB, c, c), lambda b, h, ci: (h, 0, 0)), pl.BlockSpec((c, kw), lane),
                pl.BlockSpec((c, kw), lane), pl.BlockSpec((1, vw), lane)]
    operands = [q, k, v, g, gain.reshape(1, RET_VD), dmat, qdec, kdec, cdec]
    body, aliases = _with_optional_alias(_ret_prompt_body, in_specs, operands, st_prev, out_index=1)
    return pl.pallas_call(
        body,
        grid=(BATCH, RET_HEADS // RET_HB, nc),
        in_specs=in_specs,
        out_specs=[pl.BlockSpec((c, vw), row),
                   pl.BlockSpec((None, None, RET_HB, RET_DK, RET_DV),
                                lambda b, h, ci: (layer_j, b, h, 0, 0))],
        out_shape=[jax.ShapeDtypeStruct((M_ROWS, RET_VD), BF16),
                   jax.ShapeDtypeStruct((N_RET, BATCH, RET_HEADS, RET_DK, RET_DV), F32)],
        input_output_aliases=aliases,
        compiler_params=_params("parallel", "parallel", "arbitrary"), name="ret_prompt",
    )(*operands)


def _ret_sample_body(q_ref, k_ref, v_ref, g_ref, gain_ref, dmat_ref, qdec_ref, kdec_ref, cdec_ref,
                     s0_ref, o_prev_ref, st_prev_ref, o_ref, s_ref):
    del o_prev_ref, st_prev_ref
    units = [(bi, h) for bi in range(RET_S_BB) for h in range(RET_S_HB)]
    rows = lambda bi: slice(bi * DEC_SEQ, (bi + 1) * DEC_SEQ)
    ks = lambda h: slice(h * RET_DK, (h + 1) * RET_DK)
    vs = lambda h: slice(h * RET_DV, (h + 1) * RET_DV)
    v_all = v_ref[...].astype(F32)
    q = [q_ref[rows(bi), ks(h)] for bi, h in units]
    k = [k_ref[rows(bi), ks(h)] for bi, h in units]
    v = [v_all[rows(bi), vs(h)].astype(BF16) for bi, h in units]
    a = [(lax.dot_general(q[u].astype(BF16), k[u].astype(BF16), (((1,), (1,)), ((), ())),
                          preferred_element_type=F32) * dmat_ref[h]).astype(BF16)
         for u, (bi, h) in enumerate(units)]
    o = [jnp.dot(a[u], v[u], preferred_element_type=F32)
         + jnp.dot((q[u] * qdec_ref[h]).astype(BF16), s0_ref[0, bi, h].astype(BF16),
                   preferred_element_type=F32) for u, (bi, h) in enumerate(units)]
    for u, (bi, h) in enumerate(units):
        s_ref[0, bi, h] = cdec_ref[h] * s0_ref[0, bi, h] + lax.dot_general(
            (k[u] * kdec_ref[h]).astype(BF16), v[u], (((0,), (0,)), ((), ())), preferred_element_type=F32)
    gated = [_group_norm_gate(o[u], gain_ref[:, vs(h)], g_ref[rows(bi), vs(h)])
             for u, (bi, h) in enumerate(units)]
    o_ref[...] = jnp.concatenate(
        [jnp.concatenate(gated[bi * RET_S_HB:(bi + 1) * RET_S_HB], axis=1) for bi in range(RET_S_BB)],
        axis=0).astype(BF16)


def _ret_sample(q, k, v, g, gain, state_all, layer_j, o_prev, st_prev):
    c = DEC_SEQ
    dmat, qdec, kdec, cdec = _ret_decay_tables(c)
    rows_per = RET_S_BB * DEC_SEQ
    row0 = MP_ROWS // rows_per
    n_hb = RET_HEADS // RET_S_HB
    row = lambda bb, hb: (row0 + bb, hb)
    hsel = lambda bb, hb: (hb, 0, 0)
    st_block = (1, RET_S_BB, RET_S_HB, RET_DK, RET_DV)
    st_map = lambda bb, hb: (layer_j, bb, hb, 0, 0)
    in_specs = [pl.BlockSpec((rows_per, RET_S_HB * RET_DK), row),
                pl.BlockSpec((rows_per, RET_S_HB * RET_DK), row),
                pl.BlockSpec((rows_per, RET_S_HB * RET_DV), row),
                pl.BlockSpec((rows_per, RET_S_HB * RET_DV), row),
                pl.BlockSpec((1, RET_S_HB * RET_DV), lambda bb, hb: (0, hb)),
                pl.BlockSpec((RET_S_HB, c, c), hsel),
                pl.BlockSpec((RET_S_HB, c, RET_DK), hsel),
                pl.BlockSpec((RET_S_HB, c, RET_DK), hsel),
                pl.BlockSpec((RET_S_HB, 1, RET_DV), hsel),
                pl.BlockSpec(st_block, st_map),
                pl.BlockSpec(memory_space=pl.ANY)]
    operands = [q, k, v, g, gain.reshape(1, RET_VD), dmat, qdec, kdec, cdec, state_all, o_prev]
    aliases = {10: 0}
    if st_prev is not None:
        in_specs.append(pl.BlockSpec(memory_space=pl.ANY))
        operands.append(st_prev)
        aliases[11] = 1
        body = _ret_sample_body
    else:
        body = lambda *refs: _ret_sample_body(*refs[:11], None, *refs[11:])
    return pl.pallas_call(
        body,
        grid=(DEC_BATCH // RET_S_BB, n_hb),
        in_specs=in_specs,
        out_specs=[pl.BlockSpec((rows_per, RET_S_HB * RET_DV), row), pl.BlockSpec(st_block, st_map)],
        out_shape=[jax.ShapeDtypeStruct((M_ROWS, RET_VD), BF16),
                   jax.ShapeDtypeStruct((N_RET, DEC_BATCH, RET_HEADS, RET_DK, RET_DV), F32)],
        input_output_aliases=aliases,
        compiler_params=_params("parallel", "parallel"), name="ret_sample",
    )(*operands)


def _rms_norm_gate(o, gain, gate):
    return o * lax.rsqrt(jnp.mean(o * o, axis=-1, keepdims=True) + LN_EPS) * gain * gate


def _diag_tiles(g2_blk, q_blk, g2_row, k_row):
    n = g2_blk.shape[0]
    return [jnp.exp2(g2_blk - g2_row(j)) * q_blk * k_row(j) for j in range(n)]


def _select_diag(sums, n, lane_base):
    lanes = sums.shape[1]
    rel = lax.broadcasted_iota(jnp.int32, (n, lanes), 1) - lane_base
    row = lax.broadcasted_iota(jnp.int32, (n, lanes), 0)
    acc = jnp.zeros((n, lanes), F32)
    for j in range(n):
        acc = jnp.where(rel == j, sums[j * n:(j + 1) * n, :], acc)
    return jnp.where(row >= rel, acc, 0.0)


def _gla_prompt_body(q_ref, k_ref, lf_ref, v_ref, g_ref, gain_ref, ltri_ref, st_prev_ref,
                     o_ref, st_ref, s_scr, g_scr):
    del st_prev_ref
    ci = pl.program_id(2)
    c = GLA_C
    n = GLA_SUB

    @pl.when(ci == 0)
    def _():
        s_scr[...] = jnp.zeros_like(s_scr)

    width = q_ref.shape[1]
    heads = [slice(hh * HG_DK, (hh + 1) * HG_DK) for hh in range(GLA_HB)]
    ones_b = jnp.ones((HG_DK, c), BF16)

    g2 = LOG2_E * jnp.dot(ltri_ref[...], lf_ref[...], precision=lax.Precision.HIGHEST,
                          preferred_element_type=F32)
    g_scr[...] = g2
    q = q_ref[...]
    k = k_ref[...]

    row = lax.broadcasted_iota(jnp.int32, (c, width), 0)
    ri = lax.broadcasted_iota(jnp.int32, (c, c), 0)
    cj = lax.broadcasted_iota(jnp.int32, (c, c), 1)
    a = [jnp.zeros((c, c), F32) for _ in heads]
    s = c // 2
    while s >= n:
        ref_rows = [jnp.broadcast_to(g_scr[m + s - 1:m + s, :], (2 * s, width)) for m in range(0, c, 2 * s)]
        ref = ref_rows[0] if len(ref_rows) == 1 else jnp.concatenate(ref_rows, axis=0)
        e = jnp.exp2(-jnp.abs(g2 - ref))
        second = (row & s) != 0
        qt = jnp.where(second, q * e, 0.0).astype(BF16)
        kt = jnp.where(second, 0.0, k * e).astype(BF16)
        for hh, sl in enumerate(heads):
            p = lax.dot_general(qt[:, sl], kt[:, sl], (((1,), (1,)), ((), ())), preferred_element_type=F32)
            if 2 * s < c:
                sh = int(math.log2(2 * s))
                p = jnp.where((ri >> sh) == (cj >> sh), p, 0.0)
            a[hh] = a[hh] + p
        s //= 2

    tiles = []
    for sl in heads:
        for m in range(0, c, n):
            tiles += _diag_tiles(g2[m:m + n, sl], q[m:m + n, sl],
                                 lambda j, m=m, sl=sl: g_scr[m + j:m + j + 1, sl],
                                 lambda j, m=m, sl=sl: k_ref[m + j:m + j + 1, sl])
    sums = jnp.dot(jnp.concatenate(tiles, axis=0).astype(BF16), ones_b, preferred_element_type=F32)

    qe = (q * jnp.exp2(g2)).astype(BF16)
    g2_last = g_scr[c - 1:c, :]
    kd = (k * jnp.exp2(g2_last - g2)).astype(BF16)
    decay = jnp.exp2(g2_last)
    outs = []
    for hh, sl in enumerate(heads):
        base = hh * c * n
        diag = [_select_diag(sums[base + m * n:base + (m + n) * n, :], n, m) for m in range(0, c, n)]
        a_h = (a[hh] + jnp.concatenate(diag, axis=0)).astype(BF16)
        v = v_ref[:, sl]
        s_t = s_scr[hh]
        o = jnp.dot(a_h, v, preferred_element_type=F32)
        o = o + lax.dot_general(qe[:, sl], s_t.astype(BF16), (((1,), (1,)), ((), ())),
                                preferred_element_type=F32)
        s_scr[hh] = decay[:, sl] * s_t + lax.dot_general(
            v, kd[:, sl], (((0,), (0,)), ((), ())), preferred_element_type=F32)
        outs.append(_rms_norm_gate(o, gain_ref[...], g_ref[:, sl]))

    o_ref[...] = jnp.concatenate(outs, axis=1).astype(BF16)

    @pl.when(ci == pl.num_programs(2) - 1)
    def _():
        for hh in range(GLA_HB):
            st_ref[hh] = s_scr[hh].T


def _gla_prompt(q, k, lf, v, g, gain, layer_j, st_prev):
    c = GLA_C
    nc = SEQ // c
    w = GLA_HB * HG_DK
    ltri = (jnp.arange(c)[:, None] >= jnp.arange(c)[None, :]).astype(F32)
    row = lambda b, hb, ci: (b * nc + ci, hb)
    in_specs = [pl.BlockSpec((c, w), row), pl.BlockSpec((c, w), row), pl.BlockSpec((c, w), row),
                pl.BlockSpec((c, w), row), pl.BlockSpec((c, w), row),
                pl.BlockSpec((1, HG_DV), lambda b, hb, ci: (0, 0)),
                pl.BlockSpec((c, c), lambda b, hb, ci: (0, 0))]
    operands = [q, k, lf, v, g, gain.reshape(1, HG_DV), ltri]
    body, aliases = _with_optional_alias(_gla_prompt_body, in_specs, operands, st_prev, out_index=1)
    return pl.pallas_call(
        body,
        grid=(BATCH, HG_HEADS // GLA_HB, nc),
        in_specs=in_specs,
        out_specs=[pl.BlockSpec((c, w), row),
                   pl.BlockSpec((None, None, GLA_HB, HG_DK, HG_DV),
                                lambda b, hb, ci: (layer_j, b, hb, 0, 0))],
        out_shape=[jax.ShapeDtypeStruct((M_ROWS, D_MODEL), BF16),
                   jax.ShapeDtypeStruct((N_HGRN, BATCH, HG_HEADS, HG_DK, HG_DV), F32)],
        input_output_aliases=aliases,
        scratch_shapes=[pltpu.VMEM((GLA_HB, HG_DV, HG_DK), F32), pltpu.VMEM((c, w), F32)],
        compiler_params=_params("parallel", "parallel", "arbitrary"), name="gla_prompt",
    )(*operands)


def _gla_sample_body(q_ref, k_ref, lf_ref, v_ref, g_ref, gain_ref, s0_ref, o_prev_ref, st_prev_ref,
                     o_ref, s_ref, g_scr):
    del o_prev_ref, st_prev_ref
    n = DEC_SEQ
    rows_per = SAMPLE_BB * n
    width = q_ref.shape[1]
    rix = lax.broadcasted_iota(jnp.int32, (rows_per, width), 0) & (n - 1)
    g2_all = lf_ref[...]
    sh = 1
    while sh < n:
        g2_all = g2_all + jnp.where(rix >= sh, pltpu.roll(g2_all, sh, axis=0), 0.0)
        sh *= 2
    g2_all = LOG2_E * g2_all
    g_scr[...] = g2_all
    q_all = q_ref[...]
    g2_last = jnp.concatenate([jnp.broadcast_to(g_scr[(bi + 1) * n - 1:(bi + 1) * n, :], (n, width))
                               for bi in range(SAMPLE_BB)], axis=0)
    qe_all = q_all * jnp.exp2(g2_all)
    kd_all = k_ref[...] * jnp.exp2(g2_last - g2_all)
    decay_all = jnp.exp2(g2_last)
    v_all = v_ref[...].astype(F32)
    ones_b = jnp.ones((HG_DK, HG_DK), BF16)
    zpad = jnp.zeros((HG_DK - n, HG_DV), F32)
    units = [(bi, h) for bi in range(SAMPLE_BB) for h in range(HG_HEADS)]
    span = lambda bi, h: (slice(bi * n, (bi + 1) * n), slice(h * HG_DK, (h + 1) * HG_DK))

    tiles = []
    for bi, h in units:
        rows, sl = span(bi, h)
        tiles += _diag_tiles(g2_all[rows, sl], q_all[rows, sl],
                             lambda j, bi=bi, sl=sl: g_scr[bi * n + j:bi * n + j + 1, sl],
                             lambda j, bi=bi, sl=sl: k_ref[bi * n + j:bi * n + j + 1, sl])
    sums = jnp.dot(jnp.concatenate(tiles, axis=0).astype(BF16), ones_b, preferred_element_type=F32)

    v_pads, o_units = [], []
    for u, (bi, h) in enumerate(units):
        rows, sl = span(bi, h)
        a = _select_diag(sums[u * n * n:(u + 1) * n * n, :], n, 0)
        v_pad = jnp.concatenate([v_all[rows, sl], zpad], axis=0).astype(BF16)
        v_pads.append(v_pad)
        o = jnp.dot(a.astype(BF16), v_pad, preferred_element_type=F32)
        o_units.append(o + jnp.dot(qe_all[rows, sl].astype(BF16), s0_ref[0, bi, h].astype(BF16),
                                   preferred_element_type=F32))
    for u, (bi, h) in enumerate(units):
        rows, sl = span(bi, h)
        kd_pad = jnp.concatenate([kd_all[rows, sl], zpad], axis=0).astype(BF16)
        decay_col = jnp.broadcast_to(decay_all[bi * n:bi * n + 1, sl], (HG_DV, HG_DK)).T
        s_ref[0, bi, h] = decay_col * s0_ref[0, bi, h] + lax.dot_general(
            kd_pad, v_pads[u], (((0,), (0,)), ((), ())), preferred_element_type=F32)
    outs = []
    for bi in range(SAMPLE_BB):
        per_head = [_rms_norm_gate(o_units[bi * HG_HEADS + h], gain_ref[...], g_ref[span(bi, h)])
                    for h in range(HG_HEADS)]
        outs.append(jnp.concatenate(per_head, axis=1))
    o_ref[...] = jnp.concatenate(outs, axis=0).astype(BF16)


def _gla_sample(q, k, lf, v, g, gain, state_all, layer_j, o_prev, st_prev):
    rows_per = SAMPLE_BB * DEC_SEQ
    row0 = MP_ROWS // rows_per
    row = lambda bb: (row0 + bb, 0)
    st_block = (1, SAMPLE_BB, HG_HEADS, HG_DK, HG_DV)
    st_map = lambda bb: (layer_j, bb, 0, 0, 0)
    in_specs = [pl.BlockSpec((rows_per, D_MODEL), row)] * 5 + [
        pl.BlockSpec((1, HG_DV), lambda bb: (0, 0)),
        pl.BlockSpec(st_block, st_map),
        pl.BlockSpec(memory_space=pl.ANY)]
    operands = [q, k, lf, v, g, gain.reshape(1, HG_DV), state_all, o_prev]
    aliases = {7: 0}
    if st_prev is not None:
        in_specs.append(pl.BlockSpec(memory_space=pl.ANY))
        operands.append(st_prev)
        aliases[8] = 1
        body = _gla_sample_body
    else:
        body = lambda *refs: _gla_sample_body(*refs[:8], None, *refs[8:])
    return pl.pallas_call(
        body,
        grid=(DEC_BATCH // SAMPLE_BB,),
        in_specs=in_specs,
        out_specs=[pl.BlockSpec((rows_per, D_MODEL), row), pl.BlockSpec(st_block, st_map)],
        out_shape=[jax.ShapeDtypeStruct((M_ROWS, D_MODEL), BF16),
                   jax.ShapeDtypeStruct((N_HGRN, DEC_BATCH, HG_HEADS, HG_DK, HG_DV), F32)],
        input_output_aliases=aliases,
        scratch_shapes=[pltpu.VMEM((rows_per, D_MODEL), F32)],
        compiler_params=_params("parallel"), name="gla_sample",
    )(*operands)


def _stack_rows_body(xp_ref, xs_ref, of_ref, ob_ref, *, n_prompt_blocks):
    i = pl.program_id(0)

    @pl.when(i < n_prompt_blocks)
    def _():
        of_ref[...] = xp_ref[...]
        ob_ref[...] = xp_ref[...].astype(BF16)

    @pl.when(i >= n_prompt_blocks)
    def _():
        of_ref[...] = xs_ref[...]
        ob_ref[...] = xs_ref[...].astype(BF16)


def _stack_rows(xp, xs):
    rows = STACK_ROWS
    n_p, n_s = xp.shape[0] // rows, xs.shape[0] // rows
    assert xp.shape[0] % rows == 0 and xs.shape[0] % rows == 0
    d = xp.shape[1]
    return pl.pallas_call(
        functools.partial(_stack_rows_body, n_prompt_blocks=n_p),
        grid=(n_p + n_s,),
        in_specs=[pl.BlockSpec((rows, d), lambda i: (jnp.minimum(i, n_p - 1), 0)),
                  pl.BlockSpec((rows, d), lambda i: (jnp.maximum(i - n_p, 0), 0))],
        out_specs=[pl.BlockSpec((rows, d), lambda i: (i, 0)), pl.BlockSpec((rows, d), lambda i: (i, 0))],
        out_shape=[jax.ShapeDtypeStruct((n_p * rows + n_s * rows, d), F32),
                   jax.ShapeDtypeStruct((n_p * rows + n_s * rows, d), BF16)],
        compiler_params=_params("arbitrary"), name="stack_rows",
    )(xp, xs)


def _rope_tables():
    inv = jnp.repeat(1.0 / (ROPE_BASE ** jnp.linspace(0.0, 1.0, RET_DK // 2, dtype=F32)), 2)
    pos = jnp.concatenate([0 + jnp.arange(SEQ, dtype=F32),
                           jnp.tile(PAST_LEN + jnp.arange(DEC_SEQ, dtype=F32), DEC_BATCH)])
    ang = pos[:, None] * inv[None, :]
    sign = jnp.tile(jnp.array([-1.0, 1.0], F32), RET_DK // 2)
    return jnp.cos(ang), jnp.sin(ang) * sign[None, :]


def _rope_block(j, i):
    per_seq = SEQ // TM
    return (jnp.where(i < MP_ROWS // TM, i % per_seq, per_seq + i - MP_ROWS // TM), 0)


def _retention_layer(xb, w_in, gn_gain, state_all, layer_j, st_prev, rope):
    cos, sin = rope
    rope_specs = [pl.BlockSpec((TM, RET_DK), _rope_block)] * 2
    (q,) = _dense(xb, w_in, layer_j, [0], RET_KD, functools.partial(_ep_rotary, scale=1.0), [F32],
                  extras=(cos, sin), extra_specs=rope_specs, tn=TN_WIDE, name="ret_q")
    (k,) = _dense(xb, w_in, layer_j, [RET_KD], RET_KD,
                  functools.partial(_ep_rotary, scale=RET_DK ** -0.5), [F32],
                  extras=(cos, sin), extra_specs=rope_specs, tn=TN_WIDE, name="ret_k")
    (v,) = _dense(xb, w_in, layer_j, [2 * RET_KD], RET_VD, _ep_plain, [BF16], tn=TN_WIDE,
                  hoist_cast=False, name="ret_v")
    (g,) = _dense(xb, w_in, layer_j, [2 * RET_KD + RET_VD], RET_VD, _ep_silu, [F32], tn=TN_WIDE,
                  name="ret_g")
    o, st_p = _ret_prompt(q, k, v, g, gn_gain, layer_j, st_prev[0])
    o, st_s = _ret_sample(q, k, v, g, gn_gain, state_all, layer_j, o, st_prev[1])
    return o, (st_p, st_s)


def _hgrn_layer(xb, w_in, lb_logits, norm_gain, state_all, layer_j, st_prev):
    d = D_MODEL
    logit_spec = [pl.BlockSpec((N_HGRN, TN_WIDE), lambda j, i: (0, j))]
    (q,) = _dense(xb, w_in, layer_j, [0], d, functools.partial(_ep_silu_scaled, scale=HG_DK ** -0.5),
                  [F32], tn=TN_WIDE, name="hg_q")
    lf, k = _dense(xb, w_in, layer_j, [d], d, functools.partial(_ep_forget, layer=layer_j), [F32, F32],
                   extras=(lb_logits,), extra_specs=logit_spec, tn=TN_WIDE, name="hg_f")
    (v,) = _dense(xb, w_in, layer_j, [2 * d], d, _ep_plain, [BF16], tn=TN_WIDE, hoist_cast=False,
                  name="hg_v")
    (g,) = _dense(xb, w_in, layer_j, [3 * d], d, _ep_silu, [F32], tn=TN_WIDE, name="hg_g")
    o, st_p = _gla_prompt(q, k, lf, v, g, norm_gain, layer_j, st_prev[0])
    o, st_s = _gla_sample(q, k, lf, v, g, norm_gain, state_all, layer_j, o, st_prev[1])
    return o, (st_p, st_s)


def kernel(x_prompt, x_sample, state_ret, state_hgrn, ret_w_in, ret_gn_gain, ret_w_out,
           hgrn_w_in, hgrn_lb_logits, hgrn_norm_gain, hgrn_w_out,
           ln_mix_g, ln_mix_b, ffn_w_in, ffn_w_out, ln_ffn_g, ln_ffn_b):
    assert x_prompt.shape == (BATCH, SEQ, D_MODEL) and x_sample.shape == (DEC_BATCH, DEC_SEQ, D_MODEL)
    assert state_ret.shape == (N_RET, DEC_BATCH, RET_HEADS, RET_DK, RET_DV)
    assert state_hgrn.shape == (N_HGRN, DEC_BATCH, HG_HEADS, HG_DK, HG_DV)
    assert ffn_w_in.shape == (DEPTH, D_MODEL, 2 * D_FF)

    x, xb = _stack_rows(x_prompt.reshape(MP_ROWS, D_MODEL), x_sample.reshape(MS_ROWS, D_MODEL))
    rope = _rope_tables()
    ret_w_out_b = ret_w_out.astype(BF16)
    hgrn_w_out_b = hgrn_w_out.astype(BF16)
    ffn_w_out_b = ffn_w_out.astype(BF16)

    ret_st = (None, None)
    hg_st = (None, None)
    for layer in range(DEPTH):
        j = layer // 2
        if layer % 2 == 0:
            o, ret_st = _retention_layer(xb, ret_w_in, ret_gn_gain[j], state_ret, j, ret_st, rope)
            w_out = ret_w_out_b
        else:
            o, hg_st = _hgrn_layer(xb, hgrn_w_in, hgrn_lb_logits, hgrn_norm_gain[j], state_hgrn, j, hg_st)
            w_out = hgrn_w_out_b
        x, xb = _dense_ln(o, w_out, j, x, ln_mix_g[layer], ln_mix_b[layer], name="mix_out")
        (h,) = _dense(xb, ffn_w_in, layer, [0, D_FF], D_FF, _ep_swiglu, [BF16], tm=TM_TALL,
                      hoist_cast=False, name="ffn_in")
        x, xb = _dense_ln(h, ffn_w_out_b, layer, x, ln_ffn_g[layer], ln_ffn_b[layer], name="ffn_out")

    return (x[:MP_ROWS].reshape(BATCH, SEQ, D_MODEL),
            x[MP_ROWS:].reshape(DEC_BATCH, DEC_SEQ, D_MODEL),
            ret_st[0], ret_st[1], hg_st[0], hg_st[1])
```
